```python
import math
import jax, jax.numpy as jnp
from jax import lax
import numpy as np

D_MODEL = 1024
BATCH = 4
SEQ = 8192
DEPTH = 2
DEC_BATCH = 128
DEC_SEQ = 4
PAST_LEN = 16384
PAGE_SIZE = 128

N_MIXERS = 2
EPS = 1e-5
ROPE_THETA = 500000.0
Q_BLOCK = 128
H_A = 8
HD_A = 64
VD_A = 2 * HD_A
KV_A = 2
G_A = H_A // KV_A
Q_W_A = H_A * 2 * HD_A
K_W_A = KV_A * 2 * HD_A
V_W_A = KV_A * VD_A
GATE_W_A = H_A * VD_A
SCALE_A = HD_A ** -0.5
LAMBDA_INIT_A = 0.8 - 0.6 * math.exp(-0.3 * 0)
H_B = 16
HD_B = 64
KV_B = 4
G_B = H_B // KV_B
WINDOW = 128
Q_W_B = H_B * HD_B
K_W_B = KV_B * HD_B
V_W_B = KV_B * HD_B
GATE_W_B = H_B * HD_B
SCALE_B = HD_B ** -0.5
ROT_FRAC_DEN = 4

kernel_name = "hybrid_diffattn_swa_sink_decoder_step"


def rmsnorm(x, g):
    xf = x.astype(jnp.float32)
    y = xf * lax.rsqrt(jnp.mean(xf * xf, axis=-1, keepdims=True) + EPS)
    return (y * g.astype(jnp.float32)).astype(x.dtype)


def rope(x, pos):
    hd = x.shape[-1]
    rot = hd // ROT_FRAC_DEN
    half = rot // 2
    inv = jnp.power(ROPE_THETA, -jnp.arange(half, dtype=jnp.float32) * 2.0 / rot)
    ang = pos.astype(jnp.float32)[:, None] * inv[None, :]
    shape = (1, pos.shape[0]) + (1,) * (x.ndim - 3) + (half,)
    cos = jnp.cos(ang).reshape(shape)
    sin = jnp.sin(ang).reshape(shape)
    x1 = x[..., :half].astype(jnp.float32)
    x2 = x[..., half:rot].astype(jnp.float32)
    r = jnp.concatenate([x1 * cos - x2 * sin, x2 * cos + x1 * sin], axis=-1)
    return jnp.concatenate([r.astype(x.dtype), x[..., rot:]], axis=-1)


def sink_softmax(s, mask, sink):
    s = jnp.where(mask, s, -jnp.inf)
    m = jnp.maximum(jnp.max(s, axis=-1), sink)
    e = jnp.exp(s - m[..., None])
    return e / (jnp.sum(e, axis=-1) + jnp.exp(sink - m))[..., None]


def proj_a(x, norm_g, w_in, pos):
    B, T, _ = x.shape
    z = rmsnorm(x, norm_g) @ w_in
    q = z[..., :Q_W_A].reshape(B, T, KV_A, G_A, 2, HD_A)
    k = z[..., Q_W_A:Q_W_A + K_W_A].reshape(B, T, KV_A, 2, HD_A)
    v = z[..., Q_W_A + K_W_A:Q_W_A + K_W_A + V_W_A].reshape(B, T, KV_A, VD_A)
    gate = z[..., Q_W_A + K_W_A + V_W_A:]
    return rope(q, pos), rope(k, pos), v, gate


def diff_attend(q, k, v, q_pos, k_pos, lam):
    s = jnp.einsum('bqhgcd,bkhcd->bhgcqk', q, k,
                   preferred_element_type=jnp.float32) * SCALE_A
    mask = k_pos[None, :] <= q_pos[:, None]
    p = jax.nn.softmax(jnp.where(mask, s, -jnp.inf), axis=-1)
    a = p[:, :, :, 0] - lam * p[:, :, :, 1]
    return jnp.einsum('bhgqk,bkhe->bqhge', a.astype(v.dtype), v)


def out_a(o, gate, subln_g, w_out):
    B, T = o.shape[:2]
    o = rmsnorm(o.reshape(B, T, H_A, VD_A), subln_g) * (1.0 - LAMBDA_INIT_A)
    return (o.reshape(B, T, H_A * VD_A) * jax.nn.silu(gate)) @ w_out


def diff_layer(x_p, x_s, cache_k, cache_v, page_table, norm_g, w_in,
               lq1, lk1, lq2, lk2, subln_g, w_out):
    f32 = jnp.float32
    lam = (jnp.exp(jnp.sum(lq1.astype(f32) * lk1.astype(f32)))
           - jnp.exp(jnp.sum(lq2.astype(f32) * lk2.astype(f32))) + LAMBDA_INIT_A)
    B, S, _ = x_p.shape
    pos_p = jnp.arange(S)
    q, k, v, g = proj_a(x_p, norm_g, w_in, pos_p)
    nb = S // Q_BLOCK
    qb = jnp.moveaxis(q.reshape((B, nb, Q_BLOCK) + q.shape[2:]), 1, 0)
    pb = pos_p.reshape(nb, Q_BLOCK)
    ob = lax.map(lambda a: diff_attend(a[0], k, v, a[1], pos_p, lam), (qb, pb))
    o_p = jnp.moveaxis(ob, 0, 1).reshape(B, S, KV_A, G_A, VD_A)
    y_p = x_p + out_a(o_p, g, subln_g, w_out)
    DB, T, _ = x_s.shape
    past = page_table.shape[1] * PAGE_SIZE
    pos_s = past + jnp.arange(T)
    qs, ks, vs, gs = proj_a(x_s, norm_g, w_in, pos_s)
    k_past = cache_k[page_table].reshape((DB, past) + cache_k.shape[2:])
    v_past = cache_v[page_table].reshape((DB, past) + cache_v.shape[2:])
    kk = jnp.concatenate([k_past.astype(ks.dtype), ks], axis=1)
    vv = jnp.concatenate([v_past.astype(vs.dtype), vs], axis=1)
    o_s = diff_attend(qs, kk, vv, pos_s, jnp.arange(past + T), lam)
    y_s = x_s + out_a(o_s, gs, subln_g, w_out)
    return y_p, y_s, k, v, ks, vs


def proj_b(x, norm_g, w_in, b_in, pos):
    B, T, _ = x.shape
    z = rmsnorm(x, norm_g) @ w_in + b_in
    q = z[..., :Q_W_B].reshape(B, T, KV_B, G_B, HD_B)
    k = z[..., Q_W_B:Q_W_B + K_W_B].reshape(B, T, KV_B, HD_B)
    v = z[..., Q_W_B + K_W_B:Q_W_B + K_W_B + V_W_B].reshape(B, T, KV_B, HD_B)
    gate = z[..., Q_W_B + K_W_B + V_W_B:]
    return rope(q, pos), rope(k, pos), v, gate


def out_b(o, gate, w_out):
    B, T = o.shape[:2]
    return (o.reshape(B, T, H_B * HD_B) * jax.nn.silu(gate)) @ w_out


def swa_layer(x_p, x_s, buf_k, buf_v, norm_g, w_in, b_in, sinks, w_out):
    sink = sinks.astype(jnp.float32).reshape(KV_B, G_B)
    B, S, _ = x_p.shape
    q, k, v, g = proj_b(x_p, norm_g, w_in, b_in, jnp.arange(S))
    nb = S // WINDOW
    qb = q.reshape(B, nb, WINDOW, KV_B, G_B, HD_B)
    kb = k.reshape(B, nb, WINDOW, KV_B, HD_B)
    vb = v.reshape(B, nb, WINDOW, KV_B, HD_B)
    shift = lambda t: jnp.concatenate([jnp.zeros_like(t[:, :1]), t[:, :-1]], axis=1)
    kk = jnp.concatenate([shift(kb), kb], axis=2)
    vv = jnp.concatenate([shift(vb), vb], axis=2)
    blk = jnp.arange(nb)[:, None] * WINDOW
    qa = blk + jnp.arange(WINDOW)[None, :]
    ka = blk + (jnp.arange(2 * WINDOW) - WINDOW)[None, :]
    diff = qa[:, :, None] - ka[:, None, :]
    mask = (diff >= 0) & (diff < WINDOW) & (ka[:, None, :] >= 0)
    s = jnp.einsum('bnqhgd,bnkhd->bnhgqk', qb, kk,
                   preferred_element_type=jnp.float32) * SCALE_B
    p = sink_softmax(s, mask[None, :, None, None], sink[None, None, :, :, None])
    o_p = jnp.einsum('bnhgqk,bnkhd->bnqhgd', p.astype(vv.dtype), vv)
    o_p = o_p.reshape(B, S, KV_B, G_B, HD_B)
    y_p = x_p + out_b(o_p, g, w_out)
    new_k_p = k[:, S - WINDOW:]
    new_v_p = v[:, S - WINDOW:]
    DB, T, _ = x_s.shape
    pos_s = PAST_LEN + jnp.arange(T)
    qs, ks, vs, gs = proj_b(x_s, norm_g, w_in, b_in, pos_s)
    k2 = jnp.concatenate([buf_k.astype(ks.dtype), ks], axis=1)
    v2 = jnp.concatenate([buf_v.astype(vs.dtype), vs], axis=1)
    kpos = PAST_LEN - WINDOW + jnp.arange(WINDOW + T)
    d2 = pos_s[:, None] - kpos[None, :]
    mask_s = (d2 >= 0) & (d2 < WINDOW)
    s2 = jnp.einsum('bqhgd,bkhd->bhgqk', qs, k2,
                    preferred_element_type=jnp.float32) * SCALE_B
    p2 = sink_softmax(s2, mask_s[None, None, None], sink[None, :, :, None])
    o_s = jnp.einsum('bhgqk,bkhd->bqhgd', p2.astype(v2.dtype), v2)
    y_s = x_s + out_b(o_s, gs, w_out)
    return y_p, y_s, new_k_p, new_v_p, k2[:, T:], v2[:, T:]


def setup_inputs(seed: int = 0) -> dict:
    key = jax.random.key(seed)
    ks = jax.random.split(key, 24)
    f32 = jnp.float32
    n_pages = PAST_LEN // PAGE_SIZE
    n_used = DEC_BATCH * n_pages
    n_pool = n_used + (n_used + 3) // 4
    nrm = lambda k, shp, sc: jax.random.normal(k, shp, f32) * sc
    page_table = jax.random.permutation(ks[0], n_pool)[:n_used].reshape(DEC_BATCH, n_pages).astype(jnp.int32)
    in_w_a = Q_W_A + K_W_A + V_W_A + GATE_W_A
    in_w_b = Q_W_B + K_W_B + V_W_B + GATE_W_B
    return {
        "x_prompt": nrm(ks[1], (BATCH, SEQ, D_MODEL), 1.0),
        "x_sample": nrm(ks[2], (DEC_BATCH, DEC_SEQ, D_MODEL), 1.0),
        "cache_a_k": nrm(ks[3], (n_pool, PAGE_SIZE, KV_A, 2, HD_A), 1.0),
        "cache_a_v": nrm(ks[4], (n_pool, PAGE_SIZE, KV_A, VD_A), 1.0),
        "page_table": page_table,
        "state_b_k": nrm(ks[5], (DEC_BATCH, WINDOW, KV_B, HD_B), 1.0),
        "state_b_v": nrm(ks[6], (DEC_BATCH, WINDOW, KV_B, HD_B), 1.0),
        "norm_a": 1.0 + nrm(ks[7], (D_MODEL,), 0.01),
        "w_in_a": nrm(ks[8], (D_MODEL, in_w_a), D_MODEL ** -0.5),
        "lambda_q1": nrm(ks[9], (HD_A,), 0.1),
        "lambda_k1": nrm(ks[10], (HD_A,), 0.1),
        "lambda_q2": nrm(ks[11], (HD_A,), 0.1),
        "lambda_k2": nrm(ks[12], (HD_A,), 0.1),
        "subln_a": 1.0 + nrm(ks[13], (VD_A,), 0.01),
        "w_out_a": nrm(ks[14], (GATE_W_A, D_MODEL), GATE_W_A ** -0.5),
        "norm_b": 1.0 + nrm(ks[15], (D_MODEL,), 0.01),
        "w_in_b": nrm(ks[16], (D_MODEL, in_w_b), D_MODEL ** -0.5),
        "b_in_b": nrm(ks[17], (in_w_b,), 0.02),
        "sinks_b": nrm(ks[18], (H_B,), 0.5),
        "w_out_b": nrm(ks[19], (GATE_W_B, D_MODEL), GATE_W_B ** -0.5),
        "norm_f": 1.0 + nrm(ks[20], (D_MODEL,), 0.01),
    }


def reference(x_prompt, x_sample, cache_a_k, cache_a_v, page_table, state_b_k, state_b_v,
              norm_a, w_in_a, lambda_q1, lambda_k1, lambda_q2, lambda_k2, subln_a, w_out_a,
              norm_b, w_in_b, b_in_b, sinks_b, w_out_b, norm_f):
    xp, xs = x_prompt, x_sample
    for layer in range(DEPTH):
        if layer % N_MIXERS == 0:
            xp, xs, a_k_p, a_v_p, a_k_s, a_v_s = diff_layer(
                xp, xs, cache_a_k, cache_a_v, page_table, norm_a, w_in_a,
                lambda_q1, lambda_k1, lambda_q2, lambda_k2, subln_a, w_out_a)
        else:
            xp, xs, b_k_p, b_v_p, b_k_s, b_v_s = swa_layer(
                xp, xs, state_b_k, state_b_v, norm_b, w_in_b, b_in_b, sinks_b, w_out_b)
    y_prompt = rmsnorm(xp, norm_f)
    y_sample = rmsnorm(xs, norm_f)
    return (y_prompt, y_sample, a_k_p, a_v_p, a_k_s, a_v_s, b_k_p, b_v_p, b_k_s, b_v_s)
```

```python
import functools
import math

import jax
import jax.numpy as jnp
from jax import lax
from jax.experimental import pallas as pl
from jax.experimental.pallas import tpu as pltpu

F32 = jnp.float32
BF16 = jnp.bfloat16

D_MODEL = 1024
PAGE_SIZE = 128
EPS = 1e-5
ROPE_THETA = 500000.0
H_A, HD_A, KV_A = 8, 64, 2
VD_A = 2 * HD_A
G_A = H_A // KV_A
Q_W_A = H_A * 2 * HD_A
K_W_A = KV_A * 2 * HD_A
V_W_A = KV_A * VD_A
LAMBDA_INIT_A = 0.8 - 0.6 * math.exp(-0.3 * 0)
H_B, HD_B, KV_B = 16, 64, 4
G_B = H_B // KV_B
WINDOW = 128
Q_W_B = H_B * HD_B
K_W_B = KV_B * HD_B
V_W_B = KV_B * HD_B
ROT = HD_A // 4
HALF = ROT // 2

LOG2E = math.log2(math.e)
Q_SCALE = (HD_A ** -0.5) * LOG2E

LANES = 128
VMEM_LIMIT = 48 * 1024 * 1024

TM = 512
TQ_A = 512
TQ_B = 512
PAGES_PER_STEP = 16

NT_DIMS = (((1,), (1,)), ((), ()))


def _cparams(sem):
    return pltpu.CompilerParams(dimension_semantics=sem, vmem_limit_bytes=VMEM_LIMIT)


def _rope_tables(pos):
    inv = jnp.power(ROPE_THETA, -jnp.arange(HALF, dtype=F32) * 2.0 / ROT)
    ang = pos.astype(F32)[:, None] * inv[None, :]
    cos, sin = jnp.cos(ang), jnp.sin(ang)
    ones = jnp.ones((pos.shape[0], HD_A - ROT), F32)
    cos_h = jnp.concatenate([cos, cos, ones], axis=1)
    sin_h = jnp.concatenate([-sin, sin, 0.0 * ones], axis=1)
    return jnp.tile(cos_h, (1, LANES // HD_A)), jnp.tile(sin_h, (1, LANES // HD_A))


def _proj_body(x_ref, g_ref, w_ref, b_ref, cos_ref, sin_ref,
               q_ref, k_ref, v_ref, kb_ref, vb_ref, gate_ref, *, q_w, k_w, v_w):
    x = x_ref[...]
    ms = jnp.mean(x * x, axis=-1, keepdims=True)
    xn = (x * lax.rsqrt(ms + EPS)) * g_ref[...]
    z = jnp.dot(xn.astype(BF16), w_ref[...], preferred_element_type=F32) + b_ref[...]
    cos = cos_ref[...]
    sin = sin_ref[...]
    lane = lax.broadcasted_iota(jnp.int32, cos.shape, 1)
    first_half = (lane & (HD_A - 1)) < HALF

    def rope(blk):
        partner = jnp.where(first_half, pltpu.roll(blk, LANES - HALF, axis=1),
                            pltpu.roll(blk, HALF, axis=1))
        return blk * cos + partner * sin

    for j in range(q_w // LANES):
        sl = slice(j * LANES, (j + 1) * LANES)
        q_ref[:, sl] = (rope(z[:, sl]) * Q_SCALE).astype(BF16)
    for j in range(k_w // LANES):
        sl = slice(j * LANES, (j + 1) * LANES)
        kr = rope(z[:, q_w + j * LANES:q_w + (j + 1) * LANES])
        k_ref[:, sl] = kr
        kb_ref[:, sl] = kr.astype(BF16)
    v = z[:, q_w + k_w:q_w + k_w + v_w]
    v_ref[...] = v
    vb_ref[...] = v.astype(BF16)
    gate_ref[...] = z[:, q_w + k_w + v_w:]


def _proj(x, norm_g, w_bf, bias, cos_t, sin_t, *, q_w, k_w, v_w):
    n = x.shape[0]
    width = w_bf.shape[1]
    gate_w = width - q_w - k_w - v_w
    n_tab = cos_t.shape[0] // TM
    row = lambda i: (i, 0)
    fixed = lambda i: (0, 0)
    tab = lambda i: (i % n_tab, 0)
    return pl.pallas_call(
        functools.partial(_proj_body, q_w=q_w, k_w=k_w, v_w=v_w),
        grid=(n // TM,),
        in_specs=[
            pl.BlockSpec((TM, D_MODEL), row),
            pl.BlockSpec((1, D_MODEL), fixed),
            pl.BlockSpec((D_MODEL, width), fixed),
            pl.BlockSpec((1, width), fixed),
            pl.BlockSpec((TM, LANES), tab),
            pl.BlockSpec((TM, LANES), tab),
        ],
        out_specs=[
            pl.BlockSpec((TM, q_w), row),
            pl.BlockSpec((TM, k_w), row),
            pl.BlockSpec((TM, v_w), row),
            pl.BlockSpec((TM, k_w), row),
            pl.BlockSpec((TM, v_w), row),
            pl.BlockSpec((TM, gate_w), row),
        ],
        out_shape=[
            jax.ShapeDtypeStruct((n, q_w), BF16),
            jax.ShapeDtypeStruct((n, k_w), F32),
            jax.ShapeDtypeStruct((n, v_w), F32),
            jax.ShapeDtypeStruct((n, k_w), BF16),
            jax.ShapeDtypeStruct((n, v_w), BF16),
            jax.ShapeDtypeStruct((n, gate_w), F32),
        ],
        compiler_params=_cparams(("parallel",)),
        name="proj",
    )(x, norm_g.reshape(1, -1), w_bf, bias.reshape(1, -1), cos_t, sin_t)


def _lambda_full(lq1_ref, lk1_ref, lq2_ref, lk2_ref):
    s1 = jnp.sum(lq1_ref[...] * lk1_ref[...], axis=1, keepdims=True)
    s2 = jnp.sum(lq2_ref[...] * lk2_ref[...], axis=1, keepdims=True)
    return jnp.exp(s1) - jnp.exp(s2) + LAMBDA_INIT_A


def _attn_a_body(qi_ref, kj_ref, q_ref, k_ref, v_ref, lq1_ref, lk1_ref, lq2_ref, lk2_ref,
                 o_ref, qs_ref, m_ref, l_ref, acc_ref, *, tq):
    step = pl.program_id(2)
    qi = qi_ref[step]
    kj = kj_ref[step]
    n_slab = 2 * G_A

    @pl.when(kj == 0)
    def _init():
        lane = lax.broadcasted_iota(jnp.int32, (tq, LANES), 1)
        for g in range(G_A):
            slab = q_ref[0, :, g * LANES:(g + 1) * LANES]
            zero = jnp.zeros_like(slab)
            qs_ref[g] = jnp.where(lane < HD_A, slab, zero)
            qs_ref[G_A + g] = jnp.where(lane >= HD_A, slab, zero)
        m_ref[...] = jnp.full(m_ref.shape, -jnp.inf, F32)
        l_ref[...] = jnp.zeros(l_ref.shape, F32)
        acc_ref[...] = jnp.zeros(acc_ref.shape, F32)

    def update(masked):
        def slab_step(i, carry):
            k = k_ref[0]
            v = v_ref[0]
            s = lax.dot_general(qs_ref[i], k, NT_DIMS, preferred_element_type=F32)
            if masked:
                row = lax.broadcasted_iota(jnp.int32, s.shape, 0)
                col = lax.broadcasted_iota(jnp.int32, s.shape, 1)
                s = jnp.where(col <= row, s, -jnp.inf)
            m_prev = m_ref[i]
            m_new = jnp.maximum(m_prev, jnp.max(s, axis=1, keepdims=True))
            alpha = jnp.exp2(m_prev - m_new)
            p = jnp.exp2(s - pltpu.repeat(m_new, tq // LANES, axis=1))
            l_ref[i] = alpha * l_ref[i] + jnp.sum(p, axis=1, keepdims=True)
            acc_ref[i] = alpha * acc_ref[i] + jnp.dot(p.astype(BF16), v,
                                                      preferred_element_type=F32)
            m_ref[i] = m_new
            return carry

        lax.fori_loop(0, n_slab, slab_step, 0, unroll=2)

    @pl.when(kj < qi)
    def _off_diagonal():
        update(False)

    @pl.when(kj == qi)
    def _diagonal():
        update(True)
        lam = _lambda_full(lq1_ref, lk1_ref, lq2_ref, lk2_ref)
        for g in range(G_A):
            o1 = acc_ref[g] / l_ref[g]
            o2 = acc_ref[G_A + g] / l_ref[G_A + g]
            o_ref[0, :, g * LANES:(g + 1) * LANES] = o1 - lam * o2


def _attn_a_prompt(q, kb, vb, lams, batch, seq):
    tq = TQ_A
    nq = seq // tq
    pairs = [(i, j) for i in range(nq) for j in range(i + 1)]
    qi_tab = jnp.asarray([p[0] for p in pairs], jnp.int32)
    kj_tab = jnp.asarray([p[1] for p in pairs], jnp.int32)
    q3 = q.reshape(batch, seq, Q_W_A)
    k3 = kb.reshape(batch, seq, K_W_A)
    v3 = vb.reshape(batch, seq, V_W_A)
    lam_spec = pl.BlockSpec((1, HD_A), lambda b, h, s, qi, kj: (0, 0))
    grid_spec = pltpu.PrefetchScalarGridSpec(
        num_scalar_prefetch=2,
        grid=(batch, KV_A, len(pairs)),
        in_specs=[
            pl.BlockSpec((1, tq, G_A * LANES), lambda b, h, s, qi, kj: (b, qi[s], h)),
            pl.BlockSpec((1, tq, LANES), lambda b, h, s, qi, kj: (b, kj[s], h)),
            pl.BlockSpec((1, tq, LANES), lambda b, h, s, qi, kj: (b, kj[s], h)),
            lam_spec, lam_spec, lam_spec, lam_spec,
        ],
        out_specs=pl.BlockSpec((1, tq, G_A * LANES), lambda b, h, s, qi, kj: (b, qi[s], h)),
        scratch_shapes=[
            pltpu.VMEM((2 * G_A, tq, LANES), BF16),
            pltpu.VMEM((2 * G_A, tq, LANES), F32),
            pltpu.VMEM((2 * G_A, tq, LANES), F32),
            pltpu.VMEM((2 * G_A, tq, LANES), F32),
        ],
    )
    return pl.pallas_call(
        functools.partial(_attn_a_body, tq=tq),
        grid_spec=grid_spec,
        out_shape=jax.ShapeDtypeStruct((batch, seq, H_A * VD_A), F32),
        compiler_params=_cparams(("parallel", "parallel", "arbitrary")),
        name="attn_a_prompt",
    )(qi_tab, kj_tab, q3, k3, v3, *lams)


def _attn_a_sample_body(pt_ref, qbd_ref, *rest, n_pg, t_new):
    k_refs = rest[:n_pg]
    v_refs = rest[n_pg:2 * n_pg]
    (knew_ref, vnew_ref, lq1_ref, lk1_ref, lq2_ref, lk2_ref,
     o_ref, m_ref, l_ref, acc_ref) = rest[2 * n_pg:]
    c = pl.program_id(1)
    qbd = qbd_ref[0]
    n_row = qbd.shape[0]

    @pl.when(c == 0)
    def _init():
        m_ref[...] = jnp.full(m_ref.shape, -jnp.inf, F32)
        l_ref[...] = jnp.zeros(l_ref.shape, F32)
        acc_ref[...] = jnp.zeros(acc_ref.shape, F32)

    def online_update(s, vs):
        m_prev = m_ref[...]
        m_new = jnp.maximum(m_prev, jnp.max(s, axis=1, keepdims=True))
        alpha = jnp.exp2(m_prev - m_new)
        p = jnp.exp2(s - pltpu.repeat(m_new, s.shape[1] // LANES, axis=1))
        l_ref[...] = alpha * l_ref[...] + jnp.sum(p, axis=1, keepdims=True)
        pv = jnp.zeros(acc_ref.shape, F32)
        for i, v in enumerate(vs):
            pv = pv + jnp.dot(p[:, i * LANES:(i + 1) * LANES].astype(BF16), v,
                              preferred_element_type=F32)
        acc_ref[...] = pltpu.repeat(alpha, 2, axis=1) * acc_ref[...] + pv
        m_ref[...] = m_new

    s = jnp.concatenate(
        [lax.dot_general(qbd, k_refs[i][0].astype(BF16), NT_DIMS, preferred_element_type=F32)
         for i in range(n_pg)], axis=1)
    online_update(s, [v_refs[i][0].astype(BF16) for i in range(n_pg)])

    @pl.when(c == pl.num_programs(1) - 1)
    def _finish():
        pad = jnp.zeros((PAGE_SIZE - knew_ref.shape[1], knew_ref.shape[2]), F32)
        kn = jnp.concatenate([knew_ref[0], pad], axis=0).astype(BF16)
        vn = jnp.concatenate([vnew_ref[0], pad], axis=0).astype(BF16)
        sn = lax.dot_general(qbd, kn, NT_DIMS, preferred_element_type=F32)
        t = lax.broadcasted_iota(jnp.int32, sn.shape, 0) & (t_new - 1)
        col = lax.broadcasted_iota(jnp.int32, sn.shape, 1)
        online_update(jnp.where(col <= t, sn, -jnp.inf), [vn])
        lam = _lambda_full(lq1_ref, lk1_ref, lq2_ref, lk2_ref)
        o = acc_ref[...] / pltpu.repeat(l_ref[...], 2, axis=1)
        half = n_row // 2
        diff = o[:half] - lam * o[half:]
        hrows = half // KV_A
        o_ref[0] = jnp.concatenate(
            [diff[h * hrows:(h + 1) * hrows, h * VD_A:(h + 1) * VD_A] for h in range(KV_A)],
            axis=0)


def _attn_a_sample(page_table, qbd, cache_k, cache_v, knew, vnew, lams):
    n_seq, n_pages = page_table.shape
    n_pg = PAGES_PER_STEP
    n_row = qbd.shape[1]
    t_new = n_row // (2 * KV_A * G_A)
    ck = cache_k.reshape(cache_k.shape[0], PAGE_SIZE, K_W_A)
    cv = cache_v.reshape(cache_v.shape[0], PAGE_SIZE, V_W_A)

    def page_spec(i):
        return pl.BlockSpec(
            (1, PAGE_SIZE, K_W_A),
            lambda s, c, pt, i=i: (pt[s * n_pages + c * n_pg + i], 0, 0))

    per_seq = lambda s, c, pt: (s, 0, 0)
    lam_spec = pl.BlockSpec((1, HD_A), lambda s, c, pt: (0, 0))
    grid_spec = pltpu.PrefetchScalarGridSpec(
        num_scalar_prefetch=1,
        grid=(n_seq, n_pages // n_pg),
        in_specs=([pl.BlockSpec((1, n_row, K_W_A), per_seq)]
                  + [page_spec(i) for i in range(n_pg)]
                  + [page_spec(i) for i in range(n_pg)]
                  + [pl.BlockSpec((1,) + knew.shape[1:], per_seq),
                     pl.BlockSpec((1,) + vnew.shape[1:], per_seq),
                     lam_spec, lam_spec, lam_spec, lam_spec]),
        out_specs=pl.BlockSpec((1, n_row // 2, VD_A), per_seq),
        scratch_shapes=[
            pltpu.VMEM((n_row, LANES), F32),
            pltpu.VMEM((n_row, LANES), F32),
            pltpu.VMEM((n_row, V_W_A), F32),
        ],
    )
    return pl.pallas_call(
        functools.partial(_attn_a_sample_body, n_pg=n_pg, t_new=t_new),
        grid_spec=grid_spec,
        out_shape=jax.ShapeDtypeStruct((n_seq, n_row // 2, VD_A), F32),
        compiler_params=_cparams(("parallel", "arbitrary")),
        name="attn_a_sample",
    )(page_table.reshape(-1), qbd, *([ck] * n_pg), *([cv] * n_pg), knew, vnew, *lams)


def _out_a_body(o_ref, gate_ref, x_ref, sg_ref, w_ref, y_ref):
    sg = sg_ref[...]
    us = []
    for j in range(H_A):
        sl = slice(j * VD_A, (j + 1) * VD_A)
        oj = o_ref[:, sl]
        ms = jnp.mean(oj * oj, axis=-1, keepdims=True)
        yj = ((oj * lax.rsqrt(ms + EPS)) * sg) * (1.0 - LAMBDA_INIT_A)
        gj = gate_ref[:, sl]
        us.append((yj * (gj * jax.nn.sigmoid(gj))).astype(BF16))
    u = jnp.concatenate(us, axis=1)
    y_ref[...] = x_ref[...] + jnp.dot(u, w_ref[...], preferred_element_type=F32)


def _out_a(o, gate, x, subln_g, w_bf):
    n = x.shape[0]
    row = lambda i: (i, 0)
    fixed = lambda i: (0, 0)
    return pl.pallas_call(
        _out_a_body,
        grid=(n // TM,),
        in_specs=[
            pl.BlockSpec((TM, D_MODEL), row),
            pl.BlockSpec((TM, D_MODEL), row),
            pl.BlockSpec((TM, D_MODEL), row),
            pl.BlockSpec((1, VD_A), fixed),
            pl.BlockSpec((D_MODEL, D_MODEL), fixed),
        ],
        out_specs=pl.BlockSpec((TM, D_MODEL), row),
        out_shape=jax.ShapeDtypeStruct((n, D_MODEL), F32),
        compiler_params=_cparams(("parallel",)),
        name="out_a",
    )(o, gate, x, subln_g.reshape(1, -1), w_bf)


def _sink_softmax(s, valid, sink_rows):
    s = jnp.where(valid, s, -jnp.inf)
    sk = sink_rows * LOG2E
    m = jnp.maximum(jnp.max(s, axis=1, keepdims=True), sk)
    e = jnp.exp2(s - pltpu.repeat(m, s.shape[1] // LANES, axis=1))
    den = jnp.sum(e, axis=1, keepdims=True) + jnp.exp2(sk - m)
    return e / pltpu.repeat(den, s.shape[1] // LANES, axis=1)


def _stack_heads(slabs, n_t):
    lane_head = lax.broadcasted_iota(jnp.int32, (n_t, K_W_B), 1) >> (HD_B.bit_length() - 1)
    return jnp.concatenate(
        [jnp.where(lane_head == h, slabs[g], jnp.zeros_like(slabs[g]))
         for h in range(KV_B) for g in range(len(slabs))], axis=0)


def _unstack_heads(pv, n_g, n_t):
    lane_head = lax.broadcasted_iota(jnp.int32, (n_t, K_W_B), 1) >> (HD_B.bit_length() - 1)
    outs = []
    for g in range(n_g):
        acc = jnp.zeros((n_t, K_W_B), F32)
        for h in range(KV_B):
            r0 = (h * n_g + g) * n_t
            acc = jnp.where(lane_head == h, pv[r0:r0 + n_t], acc)
        outs.append(acc)
    return outs


def _attn_b_body(q_ref, kc_ref, kp_ref, vc_ref, vp_ref, sink_ref, o_ref, *, tq):
    first_tile = pl.program_id(1) == 0
    w = WINDOW
    sink_rows = sink_ref[...]
    for j in range(tq // w):
        if j == 0:
            k2 = jnp.concatenate([kp_ref[0], kc_ref[0, :w]], axis=0)
            v2 = jnp.concatenate([vp_ref[0], vc_ref[0, :w]], axis=0)
        else:
            k2 = kc_ref[0, (j - 1) * w:(j + 1) * w]
            v2 = vc_ref[0, (j - 1) * w:(j + 1) * w]
        qsub = q_ref[0, j * w:(j + 1) * w, :]
        qst = _stack_heads([qsub[:, g * K_W_B:(g + 1) * K_W_B] for g in range(G_B)], w)
        s = lax.dot_general(qst, k2, NT_DIMS, preferred_element_type=F32)
        t = lax.broadcasted_iota(jnp.int32, s.shape, 0) & (w - 1)
        col = lax.broadcasted_iota(jnp.int32, s.shape, 1)
        valid = (col > t) & (col <= t + w)
        if j == 0:
            valid = valid & (col >= jnp.where(first_tile, w, 0))
        p = _sink_softmax(s, valid, sink_rows)
        pv = jnp.dot(p.astype(BF16), v2, preferred_element_type=F32)
        for g, og in enumerate(_unstack_heads(pv, G_B, w)):
            o_ref[0, j * w:(j + 1) * w, g * K_W_B:(g + 1) * K_W_B] = og


def _attn_b_prompt(q, kb, vb, sink_rows, batch, seq):
    tq = TQ_B
    r = tq // WINDOW
    q3 = q.reshape(batch, seq, Q_W_B)
    k3 = kb.reshape(batch, seq, K_W_B)
    v3 = vb.reshape(batch, seq, V_W_B)
    cur = lambda b, i: (b, i, 0)
    prev = lambda b, i: (b, jnp.maximum(i * r - 1, 0), 0)
    return pl.pallas_call(
        functools.partial(_attn_b_body, tq=tq),
        grid=(batch, seq // tq),
        in_specs=[
            pl.BlockSpec((1, tq, Q_W_B), cur),
            pl.BlockSpec((1, tq, K_W_B), cur),
            pl.BlockSpec((1, WINDOW, K_W_B), prev),
            pl.BlockSpec((1, tq, V_W_B), cur),
            pl.BlockSpec((1, WINDOW, V_W_B), prev),
            pl.BlockSpec(sink_rows.shape, lambda b, i: (0, 0)),
        ],
        out_specs=pl.BlockSpec((1, tq, Q_W_B), cur),
        out_shape=jax.ShapeDtypeStruct((batch, seq, Q_W_B), F32),
        compiler_params=_cparams(("parallel", "arbitrary")),
        name="attn_b_prompt",
    )(q3, k3, k3, v3, v3, sink_rows)


def _attn_b_sample_body(qbd_ref, kbuf_ref, vbuf_ref, knew_ref, vnew_ref, sink_ref, o_ref,
                        *, t_new):
    w = WINDOW
    pad = jnp.zeros((w - knew_ref.shape[1], knew_ref.shape[2]), F32)
    k2 = jnp.concatenate([kbuf_ref[0], knew_ref[0], pad], axis=0).astype(BF16)
    v2 = jnp.concatenate([vbuf_ref[0], vnew_ref[0], pad], axis=0).astype(BF16)
    s = lax.dot_general(qbd_ref[0], k2, NT_DIMS, preferred_element_type=F32)
    t = lax.broadcasted_iota(jnp.int32, s.shape, 0) & (t_new - 1)
    col = lax.broadcasted_iota(jnp.int32, s.shape, 1)
    valid = (col > t) & (col <= t + w)
    p = _sink_softmax(s, valid, sink_ref[...])
    pv = jnp.dot(p.astype(BF16), v2, preferred_element_type=F32)
    n_gt = s.shape[0] // KV_B
    o_ref[0] = _unstack_heads(pv, 1, n_gt)[0]


def _attn_b_sample(qbd, kbuf, vbuf, knew, vnew, sink_rows, t_new):
    n_seq, n_row, _ = qbd.shape
    per_seq = lambda s: (s, 0, 0)
    return pl.pallas_call(
        functools.partial(_attn_b_sample_body, t_new=t_new),
        grid=(n_seq,),
        in_specs=[
            pl.BlockSpec((1, n_row, K_W_B), per_seq),
            pl.BlockSpec((1, WINDOW, K_W_B), per_seq),
            pl.BlockSpec((1, WINDOW, V_W_B), per_seq),
            pl.BlockSpec((1,) + knew.shape[1:], per_seq),
            pl.BlockSpec((1,) + vnew.shape[1:], per_seq),
            pl.BlockSpec(sink_rows.shape, lambda s: (0, 0)),
        ],
        out_specs=pl.BlockSpec((1, n_row // KV_B, K_W_B), per_seq),
        out_shape=jax.ShapeDtypeStruct((n_seq, n_row // KV_B, K_W_B), F32),
        compiler_params=_cparams(("parallel",)),
        name="attn_b_sample",
    )(qbd, kbuf, vbuf, knew, vnew, sink_rows)


def _out_b_body(o_ref, gate_ref, x_ref, w_ref, nf_ref, y_ref):
    g = gate_ref[...]
    u = (o_ref[...] * (g * jax.nn.sigmoid(g))).astype(BF16)
    y = x_ref[...] + jnp.dot(u, w_ref[...], preferred_element_type=F32)
    ms = jnp.mean(y * y, axis=-1, keepdims=True)
    y_ref[...] = (y * lax.rsqrt(ms + EPS)) * nf_ref[...]


def _out_b(o, gate, x, w_bf, norm_f):
    n = x.shape[0]
    row = lambda i: (i, 0)
    fixed = lambda i: (0, 0)
    return pl.pallas_call(
        _out_b_body,
        grid=(n // TM,),
        in_specs=[
            pl.BlockSpec((TM, D_MODEL), row),
            pl.BlockSpec((TM, D_MODEL), row),
            pl.BlockSpec((TM, D_MODEL), row),
            pl.BlockSpec((D_MODEL, D_MODEL), fixed),
            pl.BlockSpec((1, D_MODEL), fixed),
        ],
        out_specs=pl.BlockSpec((TM, D_MODEL), row),
        out_shape=jax.ShapeDtypeStruct((n, D_MODEL), F32),
        compiler_params=_cparams(("parallel",)),
        name="out_b",
    )(o, gate, x, w_bf, norm_f.reshape(1, -1))


def _pad_rows(x, rows):
    return jnp.pad(x, ((0, 0), (0, rows - x.shape[1]), (0, 0)))


def kernel(x_prompt, x_sample, cache_a_k, cache_a_v, page_table, state_b_k, state_b_v,
           norm_a, w_in_a, lambda_q1, lambda_k1, lambda_q2, lambda_k2, subln_a, w_out_a,
           norm_b, w_in_b, b_in_b, sinks_b, w_out_b, norm_f):
    batch, seq, _ = x_prompt.shape
    n_seq, t_new, _ = x_sample.shape
    past = page_table.shape[1] * PAGE_SIZE
    xp = x_prompt.reshape(batch * seq, D_MODEL)
    xs = x_sample.reshape(n_seq * t_new, D_MODEL)

    cos_p, sin_p = _rope_tables(jnp.arange(seq))
    cos_s, sin_s = _rope_tables(past + jnp.arange(t_new))
    cos_s = jnp.tile(cos_s, (n_seq, 1))
    sin_s = jnp.tile(sin_s, (n_seq, 1))
    lams = [v.reshape(1, HD_A) for v in (lambda_q1, lambda_k1, lambda_q2, lambda_k2)]

    wa = w_in_a.astype(BF16)
    ba = jnp.zeros((w_in_a.shape[1],), F32)
    woa = w_out_a.astype(BF16)
    dims_a = dict(q_w=Q_W_A, k_w=K_W_A, v_w=V_W_A)
    q_p, k_p, v_p, kb_p, vb_p, gate_p = _proj(xp, norm_a, wa, ba, cos_p, sin_p, **dims_a)
    q_s, k_s, v_s, _, _, gate_s = _proj(xs, norm_a, wa, ba, cos_s, sin_s, **dims_a)

    o_p = _attn_a_prompt(q_p, kb_p, vb_p, lams, batch, seq)
    xp1 = _out_a(o_p.reshape(batch * seq, D_MODEL), gate_p, xp, subln_a, woa)

    qr = q_s.reshape(n_seq, t_new, KV_A, G_A, 2, HD_A).transpose(0, 4, 2, 3, 1, 5)
    qr = qr.reshape(n_seq, 2 * KV_A * G_A * t_new, HD_A)
    blk = (jnp.arange(KV_A)[None, :] * 2 + jnp.arange(2)[:, None]).reshape(-1)
    blk = jnp.repeat(blk, G_A * t_new)
    onehot = (blk[:, None] == jnp.arange(2 * KV_A)[None, :]).astype(BF16)
    qbd_a = (qr[:, :, None, :] * onehot[None, :, :, None]).reshape(n_seq, -1, K_W_A)
    knew_a = _pad_rows(k_s.reshape(n_seq, t_new, K_W_A), 8)
    vnew_a = _pad_rows(v_s.reshape(n_seq, t_new, V_W_A), 8)
    o_s = _attn_a_sample(page_table, qbd_a, cache_a_k, cache_a_v, knew_a, vnew_a, lams)
    o_s = o_s.reshape(n_seq, KV_A, G_A, t_new, VD_A).transpose(0, 3, 1, 2, 4)
    xs1 = _out_a(o_s.reshape(n_seq * t_new, D_MODEL), gate_s, xs, subln_a, woa)

    perm = (jnp.arange(G_B)[:, None, None] * HD_B
            + jnp.arange(KV_B)[None, :, None] * (G_B * HD_B)
            + jnp.arange(HD_B)[None, None, :]).reshape(-1)
    gate0 = Q_W_B + K_W_B + V_W_B
    cols = jnp.concatenate([perm, jnp.arange(Q_W_B, gate0), gate0 + perm])
    wb = w_in_b[:, cols].astype(BF16)
    bb = b_in_b[cols]
    wob = w_out_b[perm, :].astype(BF16)
    dims_b = dict(q_w=Q_W_B, k_w=K_W_B, v_w=V_W_B)
    qb_p, kf_p, vf_p, kbb_p, vbb_p, gb_p = _proj(xp1, norm_b, wb, bb, cos_p, sin_p, **dims_b)
    qb_s, kf_s, vf_s, _, _, gb_s = _proj(xs1, norm_b, wb, bb, cos_s, sin_s, **dims_b)

    sink_p = jnp.broadcast_to(jnp.repeat(sinks_b, WINDOW)[:, None], (H_B * WINDOW, LANES))
    ob_p = _attn_b_prompt(qb_p, kbb_p, vbb_p, sink_p, batch, seq)
    y_p = _out_b(ob_p.reshape(batch * seq, D_MODEL), gb_p, xp1, wob, norm_f)

    qr = qb_s.reshape(n_seq, t_new, G_B, KV_B, HD_B).transpose(0, 3, 2, 1, 4)
    qr = qr.reshape(n_seq, KV_B * G_B * t_new, HD_B)
    blk = jnp.repeat(jnp.arange(KV_B), G_B * t_new)
    onehot = (blk[:, None] == jnp.arange(KV_B)[None, :]).astype(BF16)
    qbd_b = (qr[:, :, None, :] * onehot[None, :, :, None]).reshape(n_seq, -1, K_W_B)
    knew_b = kf_s.reshape(n_seq, t_new, K_W_B)
    vnew_b = vf_s.reshape(n_seq, t_new, V_W_B)
    sink_s = jnp.broadcast_to(jnp.repeat(sinks_b, t_new)[:, None], (H_B * t_new, LANES))
    kbuf = state_b_k.reshape(n_seq, WINDOW, K_W_B)
    vbuf = state_b_v.reshape(n_seq, WINDOW, V_W_B)
    ob_s = _attn_b_sample(qbd_b, kbuf, vbuf, _pad_rows(knew_b, 8), _pad_rows(vnew_b, 8),
                          sink_s, t_new)
    ob_s = ob_s.reshape(n_seq, G_B, t_new, K_W_B).transpose(0, 2, 1, 3)
    y_s = _out_b(ob_s.reshape(n_seq * t_new, D_MODEL), gb_s, xs1, wob, norm_f)

    return (
        y_p.reshape(batch, seq, D_MODEL),
        y_s.reshape(n_seq, t_new, D_MODEL),
        k_p.reshape(batch, seq, KV_A, 2, HD_A),
        v_p.reshape(batch, seq, KV_A, VD_A),
        k_s.reshape(n_seq, t_new, KV_A, 2, HD_A),
        v_s.reshape(n_seq, t_new, KV_A, VD_A),
        kf_p.reshape(batch, seq, KV_B, HD_B)[:, seq - WINDOW:],
        vf_p.reshape(batch, seq, KV_B, HD_B)[:, seq - WINDOW:],
        jnp.concatenate([state_b_k[:, t_new:], knew_b.reshape(n_seq, t_new, KV_B, HD_B)], axis=1),
        jnp.concatenate([state_b_v[:, t_new:], vnew_b.reshape(n_seq, t_new, KV_B, HD_B)], axis=1),
    )
```

```python
import functools
import math

import jax
import jax.numpy as jnp
from jax import lax
from jax.experimental import pallas as pl
from jax.experimental.pallas import tpu as pltpu

F32 = jnp.float32
BF16 = jnp.bfloat16

D_MODEL = 1024
PAGE_SIZE = 128
EPS = 1e-5
ROPE_THETA = 500000.0
H_A, HD_A, KV_A = 8, 64, 2
VD_A = 2 * HD_A
G_A = H_A // KV_A
Q_W_A = H_A * 2 * HD_A
K_W_A = KV_A * 2 * HD_A
V_W_A = KV_A * VD_A
LAMBDA_INIT_A = 0.8 - 0.6 * math.exp(-0.3 * 0)
H_B, HD_B, KV_B = 16, 64, 4
G_B = H_B // KV_B
WINDOW = 128
Q_W_B = H_B * HD_B
K_W_B = KV_B * HD_B
V_W_B = KV_B * HD_B
ROT = HD_A // 4
HALF = ROT // 2

LOG2E = math.log2(math.e)
Q_SCALE = (HD_A ** -0.5) * LOG2E

LANES = 128
VMEM_LIMIT = 48 * 1024 * 1024

TM = 512
TQ_A = 512
SLABS_PER_DOT = 1
TQ_B = 512
PAGES_PER_STEP = 32

NT_DIMS = (((1,), (1,)), ((), ()))


def _cparams(sem):
    return pltpu.CompilerParams(dimension_semantics=sem, vmem_limit_bytes=VMEM_LIMIT)


def _rope_tables(pos):
    inv = jnp.power(ROPE_THETA, -jnp.arange(HALF, dtype=F32) * 2.0 / ROT)
    ang = pos.astype(F32)[:, None] * inv[None, :]
    cos, sin = jnp.cos(ang), jnp.sin(ang)
    ones = jnp.ones((pos.shape[0], HD_A - ROT), F32)
    cos_h = jnp.concatenate([cos, cos, ones], axis=1)
    sin_h = jnp.concatenate([-sin, sin, 0.0 * ones], axis=1)
    return jnp.tile(cos_h, (1, LANES // HD_A)), jnp.tile(sin_h, (1, LANES // HD_A))


def _proj_body(x_ref, g_ref, w_ref, b_ref, cos_ref, sin_ref,
               q_ref, k_ref, v_ref, kb_ref, vb_ref, gate_ref, *, q_w, k_w, v_w,
               k_layout, v_layout):
    x = x_ref[...]
    ms = jnp.mean(x * x, axis=-1, keepdims=True)
    xn = (x * lax.rsqrt(ms + EPS)) * g_ref[...]
    z = jnp.dot(xn.astype(BF16), w_ref[...], preferred_element_type=F32) + b_ref[...]
    cos = cos_ref[...]
    sin = sin_ref[...]
    lane = lax.broadcasted_iota(jnp.int32, cos.shape, 1)
    first_half = (lane & (HD_A - 1)) < HALF

    def rope(blk):
        partner = jnp.where(first_half, pltpu.roll(blk, LANES - HALF, axis=1),
                            pltpu.roll(blk, HALF, axis=1))
        return blk * cos + partner * sin

    for j in range(q_w // LANES):
        sl = slice(j * LANES, (j + 1) * LANES)
        q_ref[:, sl] = (rope(z[:, sl]) * Q_SCALE).astype(BF16)
    kr = jnp.concatenate(
        [rope(z[:, q_w + j * LANES:q_w + (j + 1) * LANES]) for j in range(k_w // LANES)], axis=1)
    v = z[:, q_w + k_w:q_w + k_w + v_w]
    kb_ref[...] = kr.astype(BF16)
    vb_ref[...] = v.astype(BF16)
    gate_ref[...] = z[:, q_w + k_w + v_w:]
    tm = kr.shape[0]
    if k_layout == "transposed":
        k_ref[0] = kr.T
    else:
        k_ref[...] = kr
    if v_layout == "transposed":
        v_ref[0] = v.T
    elif v_layout == "head_rows":
        n_h = v_w // LANES
        for h in range(n_h):
            v_ref[pl.ds(h, tm, stride=n_h), :] = v[:, h * LANES:(h + 1) * LANES]
    else:
        v_ref[...] = v


def _proj(x, norm_g, w_bf, bias, cos_t, sin_t, seq, *, q_w, k_w, v_w, k_layout, v_layout):
    n = x.shape[0]
    tm = min(TM, n, seq)
    width = w_bf.shape[1]
    gate_w = width - q_w - k_w - v_w
    n_tab = cos_t.shape[0] // tm
    per_seq = seq // tm
    row = lambda i: (i, 0)
    fixed = lambda i: (0, 0)
    tab = lambda i: (i % n_tab, 0)

    def f32_out(w, layout):
        if layout == "transposed":
            return (pl.BlockSpec((1, w, tm), lambda i: (i // per_seq, 0, i % per_seq)),
                    jax.ShapeDtypeStruct((n // seq, w, seq), F32))
        if layout == "head_rows":
            n_h = w // LANES
            return (pl.BlockSpec((tm * n_h, LANES), row),
                    jax.ShapeDtypeStruct((n * n_h, LANES), F32))
        return pl.BlockSpec((tm, w), row), jax.ShapeDtypeStruct((n, w), F32)

    k_spec, k_shape = f32_out(k_w, k_layout)
    v_spec, v_shape = f32_out(v_w, v_layout)
    return pl.pallas_call(
        functools.partial(_proj_body, q_w=q_w, k_w=k_w, v_w=v_w,
                          k_layout=k_layout, v_layout=v_layout),
        grid=(n // tm,),
        in_specs=[
            pl.BlockSpec((tm, D_MODEL), row),
            pl.BlockSpec((1, D_MODEL), fixed),
            pl.BlockSpec((D_MODEL, width), fixed),
            pl.BlockSpec((1, width), fixed),
            pl.BlockSpec((tm, LANES), tab),
            pl.BlockSpec((tm, LANES), tab),
        ],
        out_specs=[
            pl.BlockSpec((tm, q_w), row),
            k_spec,
            v_spec,
            pl.BlockSpec((tm, k_w), row),
            pl.BlockSpec((tm, v_w), row),
            pl.BlockSpec((tm, gate_w), row),
        ],
        out_shape=[
            jax.ShapeDtypeStruct((n, q_w), BF16),
            k_shape,
            v_shape,
            jax.ShapeDtypeStruct((n, k_w), BF16),
            jax.ShapeDtypeStruct((n, v_w), BF16),
            jax.ShapeDtypeStruct((n, gate_w), F32),
        ],
        compiler_params=_cparams(("parallel",)),
        name="proj",
    )(x, norm_g.reshape(1, -1), w_bf, bias.reshape(1, -1), cos_t, sin_t)


def _lambda_full(lq1_ref, lk1_ref, lq2_ref, lk2_ref):
    s1 = jnp.sum(lq1_ref[...] * lk1_ref[...], axis=1, keepdims=True)
    s2 = jnp.sum(lq2_ref[...] * lk2_ref[...], axis=1, keepdims=True)
    return jnp.exp(s1) - jnp.exp(s2) + LAMBDA_INIT_A


def _attn_a_body(qi_ref, kj_ref, q_ref, k_ref, v_ref, lq1_ref, lk1_ref, lq2_ref, lk2_ref,
                 o_ref, qs_ref, m_ref, acc_ref, *, tq):
    step = pl.program_id(2)
    qi = qi_ref[step]
    kj = kj_ref[step]
    n_slab = 2 * G_A

    @pl.when(kj == 0)
    def _init():
        lane = lax.broadcasted_iota(jnp.int32, (tq, LANES), 1)
        for g in range(G_A):
            slab = q_ref[0, :, g * LANES:(g + 1) * LANES]
            zero = jnp.zeros_like(slab)
            qs_ref[g] = jnp.where(lane < HD_A, slab, zero)
            qs_ref[G_A + g] = jnp.where(lane >= HD_A, slab, zero)
        m_ref[...] = jnp.full(m_ref.shape, -jnp.inf, F32)
        acc_ref[...] = jnp.zeros(acc_ref.shape, F32)

    def update(masked):
        k = k_ref[0]
        v_ones = jnp.concatenate([v_ref[0], jnp.ones((tq, LANES), BF16)], axis=1)
        rows = SLABS_PER_DOT * tq
        for i in range(n_slab // SLABS_PER_DOT):
            sl = slice(i * SLABS_PER_DOT, (i + 1) * SLABS_PER_DOT)
            qs = qs_ref[sl].reshape(rows, LANES)
            s = lax.dot_general(qs, k, NT_DIMS, preferred_element_type=F32)
            if masked:
                row = lax.broadcasted_iota(jnp.int32, s.shape, 0) & (tq - 1)
                col = lax.broadcasted_iota(jnp.int32, s.shape, 1)
                s = jnp.where(col <= row, s, -jnp.inf)
            m_prev = m_ref[sl].reshape(rows, LANES)
            m_new = jnp.maximum(m_prev, jnp.max(s, axis=1, keepdims=True))
            alpha = jnp.exp2(m_prev - m_new)
            p = jnp.exp2(s - pltpu.repeat(m_new, tq // LANES, axis=1)).astype(BF16)
            acc = (pltpu.repeat(alpha, 2, axis=1) * acc_ref[sl].reshape(rows, 2 * VD_A)
                   + jnp.dot(p, v_ones, preferred_element_type=F32))
            acc_ref[sl] = acc.reshape(SLABS_PER_DOT, tq, 2 * VD_A)
            m_ref[sl] = m_new.reshape(SLABS_PER_DOT, tq, LANES)

    @pl.when(kj < qi)
    def _off_diagonal():
        update(False)

    @pl.when(kj == qi)
    def _diagonal():
        update(True)
        lam = _lambda_full(lq1_ref, lk1_ref, lq2_ref, lk2_ref)
        for g in range(G_A):
            a1 = acc_ref[g]
            a2 = acc_ref[G_A + g]
            o1 = a1[:, :VD_A] / a1[:, VD_A:]
            o2 = a2[:, :VD_A] / a2[:, VD_A:]
            o_ref[0, :, g * LANES:(g + 1) * LANES] = o1 - lam * o2


def _attn_a_prompt(q, kb, vb, lams, batch, seq):
    tq = TQ_A
    nq = seq // tq
    pairs = [(i, j) for i in range(nq) for j in range(i + 1)]
    qi_tab = jnp.asarray([p[0] for p in pairs], jnp.int32)
    kj_tab = jnp.asarray([p[1] for p in pairs], jnp.int32)
    q3 = q.reshape(batch, seq, Q_W_A)
    k3 = kb.reshape(batch, seq, K_W_A)
    v3 = vb.reshape(batch, seq, V_W_A)
    lam_spec = pl.BlockSpec((1, HD_A), lambda b, h, s, qi, kj: (0, 0))
    grid_spec = pltpu.PrefetchScalarGridSpec(
        num_scalar_prefetch=2,
        grid=(batch, KV_A, len(pairs)),
        in_specs=[
            pl.BlockSpec((1, tq, G_A * LANES), lambda b, h, s, qi, kj: (b, qi[s], h)),
            pl.BlockSpec((1, tq, LANES), lambda b, h, s, qi, kj: (b, kj[s], h)),
            pl.BlockSpec((1, tq, LANES), lambda b, h, s, qi, kj: (b, kj[s], h)),
            lam_spec, lam_spec, lam_spec, lam_spec,
        ],
        out_specs=pl.BlockSpec((1, tq, G_A * LANES), lambda b, h, s, qi, kj: (b, qi[s], h)),
        scratch_shapes=[
            pltpu.VMEM((2 * G_A, tq, LANES), BF16),
            pltpu.VMEM((2 * G_A, tq, LANES), F32),
            pltpu.VMEM((2 * G_A, tq, 2 * VD_A), F32),
        ],
    )
    return pl.pallas_call(
        functools.partial(_attn_a_body, tq=tq),
        grid_spec=grid_spec,
        out_shape=jax.ShapeDtypeStruct((batch, seq, H_A * VD_A), F32),
        compiler_params=_cparams(("parallel", "parallel", "arbitrary")),
        name="attn_a_prompt",
    )(qi_tab, kj_tab, q3, k3, v3, *lams)


def _attn_a_sample_body(pt_ref, qbd_ref, *rest, n_pg, t_new):
    k_refs = rest[:n_pg]
    v_refs = rest[n_pg:2 * n_pg]
    (knew_ref, vnew_ref, lq1_ref, lk1_ref, lq2_ref, lk2_ref,
     o_ref, m_ref, l_ref, acc_ref) = rest[2 * n_pg:]
    c = pl.program_id(1)
    qbd = qbd_ref[0]
    n_row = qbd.shape[0]
    h_rows = n_row // KV_A

    @pl.when(c == 0)
    def _init():
        m_ref[...] = jnp.full(m_ref.shape, -jnp.inf, F32)
        l_ref[...] = jnp.zeros(l_ref.shape, F32)
        acc_ref[...] = jnp.zeros(acc_ref.shape, F32)

    def online_update(s, v_of):
        m_prev = m_ref[...]
        m_new = jnp.maximum(m_prev, jnp.max(s, axis=1, keepdims=True))
        alpha = jnp.exp2(m_prev - m_new)
        p = jnp.exp2(s - pltpu.repeat(m_new, s.shape[1] // LANES, axis=1))
        l_ref[...] = alpha * l_ref[...] + jnp.sum(p, axis=1, keepdims=True)
        p = p.astype(BF16)
        pvs = []
        for h in range(KV_A):
            pv = jnp.zeros((h_rows, VD_A), F32)
            for i in range(s.shape[1] // LANES):
                pv = pv + jnp.dot(p[h * h_rows:(h + 1) * h_rows, i * LANES:(i + 1) * LANES],
                                  v_of(i, h), preferred_element_type=F32)
            pvs.append(pv)
        acc_ref[...] = alpha * acc_ref[...] + jnp.concatenate(pvs, axis=0)
        m_ref[...] = m_new

    s = jnp.concatenate(
        [jnp.dot(qbd, k_refs[i][0].astype(BF16), preferred_element_type=F32)
         for i in range(n_pg)], axis=1)
    online_update(
        s, lambda i, h: v_refs[i][pl.ds(h, PAGE_SIZE, stride=KV_A), :].astype(BF16))

    @pl.when(c == pl.num_programs(1) - 1)
    def _finish():
        pad = jnp.zeros((PAGE_SIZE - knew_ref.shape[1], knew_ref.shape[2]), F32)
        kn = jnp.concatenate([knew_ref[0], pad], axis=0).astype(BF16)
        vn = jnp.concatenate([vnew_ref[0], pad], axis=0).astype(BF16)
        sn = lax.dot_general(qbd, kn, NT_DIMS, preferred_element_type=F32)
        t = lax.broadcasted_iota(jnp.int32, sn.shape, 0) & (t_new - 1)
        col = lax.broadcasted_iota(jnp.int32, sn.shape, 1)
        online_update(jnp.where(col <= t, sn, -jnp.inf),
                      lambda i, h: vn[:, h * VD_A:(h + 1) * VD_A])
        lam = _lambda_full(lq1_ref, lk1_ref, lq2_ref, lk2_ref)
        o = acc_ref[...] / l_ref[...]
        half = h_rows // 2
        o_ref[0] = jnp.concatenate(
            [o[h * h_rows:h * h_rows + half] - lam * o[h * h_rows + half:(h + 1) * h_rows]
             for h in range(KV_A)], axis=0)


def _attn_a_sample(page_table, qbd, cache_k, cache_v, knew, vnew, lams):
    n_seq, n_pages = page_table.shape
    n_pool = cache_k.shape[0]
    n_pg = PAGES_PER_STEP
    n_row = qbd.shape[1]
    t_new = n_row // (2 * KV_A * G_A)
    ck = jnp.transpose(cache_k, (0, 2, 3, 4, 1)).reshape(n_pool, K_W_A, PAGE_SIZE)
    cv = cache_v.reshape(n_pool * PAGE_SIZE * KV_A, VD_A)

    def page(c, i, s, pt):
        return pt[s * n_pages + c * n_pg + i]

    def k_spec(i):
        return pl.BlockSpec((1, K_W_A, PAGE_SIZE), lambda s, c, pt: (page(c, i, s, pt), 0, 0))

    def v_spec(i):
        return pl.BlockSpec((PAGE_SIZE * KV_A, VD_A), lambda s, c, pt: (page(c, i, s, pt), 0))

    per_seq = lambda s, c, pt: (s, 0, 0)
    lam_spec = pl.BlockSpec((1, HD_A), lambda s, c, pt: (0, 0))
    grid_spec = pltpu.PrefetchScalarGridSpec(
        num_scalar_prefetch=1,
        grid=(n_seq, n_pages // n_pg),
        in_specs=([pl.BlockSpec((1, n_row, K_W_A), per_seq)]
                  + [k_spec(i) for i in range(n_pg)]
                  + [v_spec(i) for i in range(n_pg)]
                  + [pl.BlockSpec((1,) + knew.shape[1:], per_seq),
                     pl.BlockSpec((1,) + vnew.shape[1:], per_seq),
                     lam_spec, lam_spec, lam_spec, lam_spec]),
        out_specs=pl.BlockSpec((1, n_row // 2, VD_A), per_seq),
        scratch_shapes=[
            pltpu.VMEM((n_row, LANES), F32),
            pltpu.VMEM((n_row, LANES), F32),
            pltpu.VMEM((n_row, VD_A), F32),
        ],
    )
    return pl.pallas_call(
        functools.partial(_attn_a_sample_body, n_pg=n_pg, t_new=t_new),
        grid_spec=grid_spec,
        out_shape=jax.ShapeDtypeStruct((n_seq, n_row // 2, VD_A), F32),
        compiler_params=_cparams(("parallel", "arbitrary")),
        name="attn_a_sample",
    )(page_table.reshape(-1), qbd, *([ck] * n_pg), *([cv] * n_pg), knew, vnew, *lams)


def _out_a_body(o_ref, gate_ref, x_ref, sg_ref, w_ref, y_ref):
    sg = sg_ref[...]
    us = []
    for j in range(H_A):
        sl = slice(j * VD_A, (j + 1) * VD_A)
        oj = o_ref[:, sl]
        ms = jnp.mean(oj * oj, axis=-1, keepdims=True)
        yj = ((oj * lax.rsqrt(ms + EPS)) * sg) * (1.0 - LAMBDA_INIT_A)
        gj = gate_ref[:, sl]
        us.append((yj * (gj * jax.nn.sigmoid(gj))).astype(BF16))
    u = jnp.concatenate(us, axis=1)
    y_ref[...] = x_ref[...] + jnp.dot(u, w_ref[...], preferred_element_type=F32)


def _out_a(o, gate, x, subln_g, w_bf):
    n = x.shape[0]
    tm = min(TM, n)
    row = lambda i: (i, 0)
    fixed = lambda i: (0, 0)
    return pl.pallas_call(
        _out_a_body,
        grid=(n // tm,),
        in_specs=[
            pl.BlockSpec((tm, D_MODEL), row),
            pl.BlockSpec((tm, D_MODEL), row),
            pl.BlockSpec((tm, D_MODEL), row),
            pl.BlockSpec((1, VD_A), fixed),
            pl.BlockSpec((D_MODEL, D_MODEL), fixed),
        ],
        out_specs=pl.BlockSpec((tm, D_MODEL), row),
        out_shape=jax.ShapeDtypeStruct((n, D_MODEL), F32),
        compiler_params=_cparams(("parallel",)),
        name="out_a",
    )(o, gate, x, subln_g.reshape(1, -1), w_bf)


def _sink_softmax(s, valid, sink_rows):
    s = jnp.where(valid, s, -jnp.inf)
    sk = sink_rows * LOG2E
    m = jnp.maximum(jnp.max(s, axis=1, keepdims=True), sk)
    e = jnp.exp2(s - pltpu.repeat(m, s.shape[1] // LANES, axis=1))
    den = jnp.sum(e, axis=1, keepdims=True) + jnp.exp2(sk - m)
    return e / pltpu.repeat(den, s.shape[1] // LANES, axis=1)


def _stack_heads(slabs, n_t):
    lane_head = lax.broadcasted_iota(jnp.int32, (n_t, K_W_B), 1) >> (HD_B.bit_length() - 1)
    return jnp.concatenate(
        [jnp.where(lane_head == h, slabs[g], jnp.zeros_like(slabs[g]))
         for h in range(KV_B) for g in range(len(slabs))], axis=0)


def _unstack_heads(pv, n_g, n_t):
    lane_head = lax.broadcasted_iota(jnp.int32, (n_t, K_W_B), 1) >> (HD_B.bit_length() - 1)
    outs = []
    for g in range(n_g):
        acc = jnp.zeros((n_t, K_W_B), F32)
        for h in range(KV_B):
            r0 = (h * n_g + g) * n_t
            acc = jnp.where(lane_head == h, pv[r0:r0 + n_t], acc)
        outs.append(acc)
    return outs


def _attn_b_body(q_ref, kc_ref, kp_ref, vc_ref, vp_ref, sink_ref, o_ref, *, tq):
    first_tile = pl.program_id(1) == 0
    w = WINDOW
    sink_rows = sink_ref[...]
    for j in range(tq // w):
        if j == 0:
            k2 = jnp.concatenate([kp_ref[0], kc_ref[0, :w]], axis=0)
            v2 = jnp.concatenate([vp_ref[0], vc_ref[0, :w]], axis=0)
        else:
            k2 = kc_ref[0, (j - 1) * w:(j + 1) * w]
            v2 = vc_ref[0, (j - 1) * w:(j + 1) * w]
        qsub = q_ref[0, j * w:(j + 1) * w, :]
        qst = _stack_heads([qsub[:, g * K_W_B:(g + 1) * K_W_B] for g in range(G_B)], w)
        s = lax.dot_general(qst, k2, NT_DIMS, preferred_element_type=F32)
        t = lax.broadcasted_iota(jnp.int32, s.shape, 0) & (w - 1)
        col = lax.broadcasted_iota(jnp.int32, s.shape, 1)
        valid = (col > t) & (col <= t + w)
        if j == 0:
            valid = valid & (col >= jnp.where(first_tile, w, 0))
        p = _sink_softmax(s, valid, sink_rows)
        pv = jnp.dot(p.astype(BF16), v2, preferred_element_type=F32)
        for g, og in enumerate(_unstack_heads(pv, G_B, w)):
            o_ref[0, j * w:(j + 1) * w, g * K_W_B:(g + 1) * K_W_B] = og


def _attn_b_prompt(q, kb, vb, sink_rows, batch, seq):
    tq = TQ_B
    r = tq // WINDOW
    q3 = q.reshape(batch, seq, Q_W_B)
    k3 = kb.reshape(batch, seq, K_W_B)
    v3 = vb.reshape(batch, seq, V_W_B)
    cur = lambda b, i: (b, i, 0)
    prev = lambda b, i: (b, jnp.maximum(i * r - 1, 0), 0)
    return pl.pallas_call(
        functools.partial(_attn_b_body, tq=tq),
        grid=(batch, seq // tq),
        in_specs=[
            pl.BlockSpec((1, tq, Q_W_B), cur),
            pl.BlockSpec((1, tq, K_W_B), cur),
            pl.BlockSpec((1, WINDOW, K_W_B), prev),
            pl.BlockSpec((1, tq, V_W_B), cur),
            pl.BlockSpec((1, WINDOW, V_W_B), prev),
            pl.BlockSpec(sink_rows.shape, lambda b, i: (0, 0)),
        ],
        out_specs=pl.BlockSpec((1, tq, Q_W_B), cur),
        out_shape=jax.ShapeDtypeStruct((batch, seq, Q_W_B), F32),
        compiler_params=_cparams(("parallel", "arbitrary")),
        name="attn_b_prompt",
    )(q3, k3, k3, v3, v3, sink_rows)


def _attn_b_sample_body(qbd_ref, kt_ref, vt_ref, knew_ref, vnew_ref, sink_ref,
                        o_ref, kto_ref, vto_ref, *, t_new):
    w = WINDOW
    lane = lax.broadcasted_iota(jnp.int32, (K_W_B, w), 1)

    def new_tile(new_ref):
        pad = jnp.zeros((w - new_ref.shape[1], new_ref.shape[2]), F32)
        return jnp.concatenate([pad, new_ref[0]], axis=0).T

    def shifted(buf_t, new_t):
        return jnp.where(lane < w - t_new, pltpu.roll(buf_t, w - t_new, axis=1), new_t)

    kt, vt = kt_ref[0], vt_ref[0]
    knew_t, vnew_t = new_tile(knew_ref), new_tile(vnew_ref)
    kto_ref[0] = shifted(kt, knew_t)
    vto_ref[0] = shifted(vt, vnew_t)

    k2t = jnp.concatenate([kt, knew_t], axis=1).astype(BF16)
    v2 = jnp.concatenate([vt, vnew_t], axis=1).T.astype(BF16)
    s = jnp.dot(qbd_ref[0], k2t, preferred_element_type=F32)
    t = lax.broadcasted_iota(jnp.int32, s.shape, 0) & (t_new - 1)
    col = lax.broadcasted_iota(jnp.int32, s.shape, 1)
    valid = ((col > t) & (col < w)) | ((col >= 2 * w - t_new) & (col <= t + 2 * w - t_new))
    p = _sink_softmax(s, valid, sink_ref[...])
    pv = jnp.dot(p.astype(BF16), v2, preferred_element_type=F32)
    n_gt = s.shape[0] // KV_B
    o_ref[0] = _unstack_heads(pv, 1, n_gt)[0]


def _attn_b_sample(qbd, kbuf_t, vbuf_t, knew, vnew, sink_rows, t_new):
    n_seq, n_row, _ = qbd.shape
    per_seq = lambda s: (s, 0, 0)
    buf_spec = pl.BlockSpec((1, K_W_B, WINDOW), per_seq)
    buf_shape = jax.ShapeDtypeStruct((n_seq, K_W_B, WINDOW), F32)
    return pl.pallas_call(
        functools.partial(_attn_b_sample_body, t_new=t_new),
        grid=(n_seq,),
        in_specs=[
            pl.BlockSpec((1, n_row, K_W_B), per_seq),
            buf_spec,
            buf_spec,
            pl.BlockSpec((1,) + knew.shape[1:], per_seq),
            pl.BlockSpec((1,) + vnew.shape[1:], per_seq),
            pl.BlockSpec(sink_rows.shape, lambda s: (0, 0)),
        ],
        out_specs=[pl.BlockSpec((1, n_row // KV_B, K_W_B), per_seq), buf_spec, buf_spec],
        out_shape=[jax.ShapeDtypeStruct((n_seq, n_row // KV_B, K_W_B), F32), buf_shape, buf_shape],
        compiler_params=_cparams(("parallel",)),
        name="attn_b_sample",
    )(qbd, kbuf_t, vbuf_t, knew, vnew, sink_rows)


def _out_b_body(o_ref, gate_ref, x_ref, w_ref, nf_ref, y_ref):
    g = gate_ref[...]
    u = (o_ref[...] * (g * jax.nn.sigmoid(g))).astype(BF16)
    y = x_ref[...] + jnp.dot(u, w_ref[...], preferred_element_type=F32)
    ms = jnp.mean(y * y, axis=-1, keepdims=True)
    y_ref[...] = (y * lax.rsqrt(ms + EPS)) * nf_ref[...]


def _out_b(o, gate, x, w_bf, norm_f):
    n = x.shape[0]
    tm = min(TM, n)
    row = lambda i: (i, 0)
    fixed = lambda i: (0, 0)
    return pl.pallas_call(
        _out_b_body,
        grid=(n // tm,),
        in_specs=[
            pl.BlockSpec((tm, D_MODEL), row),
            pl.BlockSpec((tm, D_MODEL), row),
            pl.BlockSpec((tm, D_MODEL), row),
            pl.BlockSpec((D_MODEL, D_MODEL), fixed),
            pl.BlockSpec((1, D_MODEL), fixed),
        ],
        out_specs=pl.BlockSpec((tm, D_MODEL), row),
        out_shape=jax.ShapeDtypeStruct((n, D_MODEL), F32),
        compiler_params=_cparams(("parallel",)),
        name="out_b",
    )(o, gate, x, w_bf, norm_f.reshape(1, -1))


def _pad_rows(x, rows):
    return jnp.pad(x, ((0, 0), (0, rows - x.shape[1]), (0, 0)))


def kernel(x_prompt, x_sample, cache_a_k, cache_a_v, page_table, state_b_k, state_b_v,
           norm_a, w_in_a, lambda_q1, lambda_k1, lambda_q2, lambda_k2, subln_a, w_out_a,
           norm_b, w_in_b, b_in_b, sinks_b, w_out_b, norm_f):
    batch, seq, _ = x_prompt.shape
    n_seq, t_new, _ = x_sample.shape
    past = page_table.shape[1] * PAGE_SIZE
    xp = x_prompt.reshape(batch * seq, D_MODEL)
    xs = x_sample.reshape(n_seq * t_new, D_MODEL)

    cos_p, sin_p = _rope_tables(jnp.arange(seq))
    cos_s, sin_s = _rope_tables(past + jnp.arange(t_new))
    cos_s = jnp.tile(cos_s, (n_seq, 1))
    sin_s = jnp.tile(sin_s, (n_seq, 1))
    lams = [v.reshape(1, HD_A) for v in (lambda_q1, lambda_k1, lambda_q2, lambda_k2)]

    wa = w_in_a.astype(BF16)
    ba = jnp.zeros((w_in_a.shape[1],), F32)
    woa = w_out_a.astype(BF16)
    dims_a = dict(q_w=Q_W_A, k_w=K_W_A, v_w=V_W_A)
    q_p, kt_p, vh_p, kb_p, vb_p, gate_p = _proj(
        xp, norm_a, wa, ba, cos_p, sin_p, seq, k_layout="transposed", v_layout="head_rows",
        **dims_a)
    q_s, k_s, vh_s, _, _, gate_s = _proj(
        xs, norm_a, wa, ba, cos_s, sin_s, n_seq * t_new, k_layout="rows", v_layout="head_rows",
        **dims_a)

    o_p = _attn_a_prompt(q_p, kb_p, vb_p, lams, batch, seq)
    xp1 = _out_a(o_p.reshape(batch * seq, D_MODEL), gate_p, xp, subln_a, woa)

    qr = q_s.reshape(n_seq, t_new, KV_A, G_A, 2, HD_A).transpose(0, 2, 4, 3, 1, 5)
    qr = qr.reshape(n_seq, 2 * KV_A * G_A * t_new, HD_A)
    blk = jnp.repeat(jnp.arange(2 * KV_A), G_A * t_new)
    onehot = (blk[:, None] == jnp.arange(2 * KV_A)[None, :]).astype(BF16)
    qbd_a = (qr[:, :, None, :] * onehot[None, :, :, None]).reshape(n_seq, -1, K_W_A)
    knew_a = _pad_rows(k_s.reshape(n_seq, t_new, K_W_A), 8)
    vnew_a = _pad_rows(vh_s.reshape(n_seq, t_new, V_W_A), 8)
    o_s = _attn_a_sample(page_table, qbd_a, cache_a_k, cache_a_v, knew_a, vnew_a, lams)
    o_s = o_s.reshape(n_seq, KV_A, G_A, t_new, VD_A).transpose(0, 3, 1, 2, 4)
    xs1 = _out_a(o_s.reshape(n_seq * t_new, D_MODEL), gate_s, xs, subln_a, woa)

    perm = (jnp.arange(G_B)[:, None, None] * HD_B
            + jnp.arange(KV_B)[None, :, None] * (G_B * HD_B)
            + jnp.arange(HD_B)[None, None, :]).reshape(-1)
    gate0 = Q_W_B + K_W_B + V_W_B
    cols = jnp.concatenate([perm, jnp.arange(Q_W_B, gate0), gate0 + perm])
    wb = w_in_b[:, cols].astype(BF16)
    bb = b_in_b[cols]
    wob = w_out_b[perm, :].astype(BF16)
    dims_b = dict(q_w=Q_W_B, k_w=K_W_B, v_w=V_W_B)
    qb_p, kt_bp, vt_bp, kbb_p, vbb_p, gb_p = _proj(
        xp1, norm_b, wb, bb, cos_p, sin_p, seq, k_layout="transposed", v_layout="transposed",
        **dims_b)
    qb_s, kf_s, vf_s, _, _, gb_s = _proj(
        xs1, norm_b, wb, bb, cos_s, sin_s, n_seq * t_new, k_layout="rows", v_layout="rows",
        **dims_b)

    sink_p = jnp.broadcast_to(jnp.repeat(sinks_b, WINDOW)[:, None], (H_B * WINDOW, LANES))
    ob_p = _attn_b_prompt(qb_p, kbb_p, vbb_p, sink_p, batch, seq)
    y_p = _out_b(ob_p.reshape(batch * seq, D_MODEL), gb_p, xp1, wob, norm_f)

    qr = qb_s.reshape(n_seq, t_new, G_B, KV_B, HD_B).transpose(0, 3, 2, 1, 4)
    qr = qr.reshape(n_seq, KV_B * G_B * t_new, HD_B)
    blk = jnp.repeat(jnp.arange(KV_B), G_B * t_new)
    onehot = (blk[:, None] == jnp.arange(KV_B)[None, :]).astype(BF16)
    qbd_b = (qr[:, :, None, :] * onehot[None, :, :, None]).reshape(n_seq, -1, K_W_B)
    sink_s = jnp.broadcast_to(jnp.repeat(sinks_b, t_new)[:, None], (H_B * t_new, LANES))
    front = ((0, 0), (8 - t_new, 0), (0, 0))
    knew_b = jnp.pad(kf_s.reshape(n_seq, t_new, K_W_B), front)
    vnew_b = jnp.pad(vf_s.reshape(n_seq, t_new, V_W_B), front)
    to_minor = lambda buf: buf.transpose(0, 2, 3, 1).reshape(buf.shape[0], -1, WINDOW)
    from_minor = lambda t: t.reshape(t.shape[0], KV_B, HD_B, WINDOW).transpose(0, 3, 1, 2)
    ob_s, kt_bs, vt_bs = _attn_b_sample(qbd_b, to_minor(state_b_k), to_minor(state_b_v),
                                        knew_b, vnew_b, sink_s, t_new)
    ob_s = ob_s.reshape(n_seq, G_B, t_new, K_W_B).transpose(0, 2, 1, 3)
    y_s = _out_b(ob_s.reshape(n_seq * t_new, D_MODEL), gb_s, xs1, wob, norm_f)

    return (
        y_p.reshape(batch, seq, D_MODEL),
        y_s.reshape(n_seq, t_new, D_MODEL),
        kt_p.reshape(batch, KV_A, 2, HD_A, seq).transpose(0, 4, 1, 2, 3),
        vh_p.reshape(batch, seq, KV_A, VD_A),
        k_s.reshape(n_seq, t_new, KV_A, 2, HD_A),
        vh_s.reshape(n_seq, t_new, KV_A, VD_A),
        from_minor(kt_bp[:, :, seq - WINDOW:]),
        from_minor(vt_bp[:, :, seq - WINDOW:]),
        from_minor(kt_bs),
        from_minor(vt_bs),
    )
```

```python
import functools
import math

import jax
import jax.numpy as jnp
from jax import lax
from jax.experimental import pallas as pl
from jax.experimental.pallas import tpu as pltpu

F32 = jnp.float32
BF16 = jnp.bfloat16

D_MODEL = 1024
PAGE_SIZE = 128
EPS = 1e-5
ROPE_THETA = 500000.0
H_A, HD_A, KV_A = 8, 64, 2
VD_A = 2 * HD_A
G_A = H_A // KV_A
Q_W_A = H_A * 2 * HD_A
K_W_A = KV_A * 2 * HD_A
V_W_A = KV_A * VD_A
LAMBDA_INIT_A = 0.8 - 0.6 * math.exp(-0.3 * 0)
H_B, HD_B, KV_B = 16, 64, 4
G_B = H_B // KV_B
WINDOW = 128
Q_W_B = H_B * HD_B
K_W_B = KV_B * HD_B
V_W_B = KV_B * HD_B
ROT = HD_A // 4
HALF = ROT // 2

LOG2E = math.log2(math.e)
Q_SCALE = (HD_A ** -0.5) * LOG2E

LANES = 128
VMEM_LIMIT = 48 * 1024 * 1024

TM = 512
TQ_A = 512
TQ_B = 512
PAGES_PER_STEP = 32
SEQS_PER_STEP_B = 8

NT_DIMS = (((1,), (1,)), ((), ()))


def _lane_tile(x, n):
    return x if n == 1 else jnp.concatenate([x] * n, axis=1)


def _cparams(sem):
    return pltpu.CompilerParams(dimension_semantics=sem, vmem_limit_bytes=VMEM_LIMIT)


def _rope_tables(pos):
    inv = jnp.power(ROPE_THETA, -jnp.arange(HALF, dtype=F32) * 2.0 / ROT)
    ang = pos.astype(F32)[:, None] * inv[None, :]
    cos, sin = jnp.cos(ang), jnp.sin(ang)
    ones = jnp.ones((pos.shape[0], HD_A - ROT), F32)
    cos_h = jnp.concatenate([cos, cos, ones], axis=1)
    sin_h = jnp.concatenate([-sin, sin, 0.0 * ones], axis=1)
    return jnp.tile(cos_h, (1, LANES // HD_A)), jnp.tile(sin_h, (1, LANES // HD_A))


def _proj_body(x_ref, g_ref, w_ref, b_ref, cos_ref, sin_ref,
               q_ref, k_ref, v_ref, kb_ref, vb_ref, gate_ref, *, q_w, k_w, v_w,
               k_layout, v_layout, vb_layout):
    x = x_ref[...]
    ms = jnp.mean(x * x, axis=-1, keepdims=True)
    xn = (x * lax.rsqrt(ms + EPS)) * g_ref[...]
    z = jnp.dot(xn.astype(BF16), w_ref[...], preferred_element_type=F32) + b_ref[...]
    cos = cos_ref[...]
    sin = sin_ref[...]
    lane = lax.broadcasted_iota(jnp.int32, cos.shape, 1)
    first_half = (lane & (HD_A - 1)) < HALF

    def rope(blk):
        partner = jnp.where(first_half, pltpu.roll(blk, LANES - HALF, axis=1),
                            pltpu.roll(blk, HALF, axis=1))
        return blk * cos + partner * sin

    for j in range(q_w // LANES):
        sl = slice(j * LANES, (j + 1) * LANES)
        q_ref[:, sl] = (rope(z[:, sl]) * Q_SCALE).astype(BF16)
    kr = jnp.concatenate(
        [rope(z[:, q_w + j * LANES:q_w + (j + 1) * LANES]) for j in range(k_w // LANES)], axis=1)
    v = z[:, q_w + k_w:q_w + k_w + v_w]
    kb_ref[...] = kr.astype(BF16)
    gate_ref[...] = z[:, q_w + k_w + v_w:].astype(gate_ref.dtype)
    tm = kr.shape[0]
    if k_layout == "transposed":
        k_ref[0] = kr.T
    else:
        k_ref[...] = kr
    if "transposed" in (v_layout, vb_layout):
        vt = v.T
    if vb_layout == "transposed":
        vb_ref[0] = vt.astype(BF16)
    else:
        vb_ref[...] = v.astype(BF16)
    if v_layout == "transposed":
        v_ref[0] = vt
    elif v_layout == "head_rows":
        n_h = v_w // LANES
        for h in range(n_h):
            v_ref[pl.ds(h, tm, stride=n_h), :] = v[:, h * LANES:(h + 1) * LANES]
    else:
        v_ref[...] = v


def _proj(x, norm_g, w_bf, bias, cos_t, sin_t, seq, *, q_w, k_w, v_w,
          k_layout, v_layout, vb_layout):
    n = x.shape[0]
    tm = min(TM, n, seq)
    width = w_bf.shape[1]
    gate_w = width - q_w - k_w - v_w
    n_tab = cos_t.shape[0] // tm
    per_seq = seq // tm
    row = lambda i: (i, 0)
    fixed = lambda i: (0, 0)
    tab = lambda i: (i % n_tab, 0)

    def out(w, layout, dtype):
        if layout == "transposed":
            return (pl.BlockSpec((1, w, tm), lambda i: (i // per_seq, 0, i % per_seq)),
                    jax.ShapeDtypeStruct((n // seq, w, seq), dtype))
        if layout == "head_rows":
            n_h = w // LANES
            return (pl.BlockSpec((tm * n_h, LANES), row),
                    jax.ShapeDtypeStruct((n * n_h, LANES), dtype))
        return pl.BlockSpec((tm, w), row), jax.ShapeDtypeStruct((n, w), dtype)

    k_spec, k_shape = out(k_w, k_layout, F32)
    v_spec, v_shape = out(v_w, v_layout, F32)
    vb_spec, vb_shape = out(v_w, vb_layout, BF16)
    return pl.pallas_call(
        functools.partial(_proj_body, q_w=q_w, k_w=k_w, v_w=v_w,
                          k_layout=k_layout, v_layout=v_layout, vb_layout=vb_layout),
        grid=(n // tm,),
        in_specs=[
            pl.BlockSpec((tm, D_MODEL), row),
            pl.BlockSpec((1, D_MODEL), fixed),
            pl.BlockSpec((D_MODEL, width), fixed),
            pl.BlockSpec((1, width), fixed),
            pl.BlockSpec((tm, LANES), tab),
            pl.BlockSpec((tm, LANES), tab),
        ],
        out_specs=[
            pl.BlockSpec((tm, q_w), row),
            k_spec,
            v_spec,
            pl.BlockSpec((tm, k_w), row),
            vb_spec,
            pl.BlockSpec((tm, gate_w), row),
        ],
        out_shape=[
            jax.ShapeDtypeStruct((n, q_w), BF16),
            k_shape,
            v_shape,
            jax.ShapeDtypeStruct((n, k_w), BF16),
            vb_shape,
            jax.ShapeDtypeStruct((n, gate_w), BF16),
        ],
        compiler_params=_cparams(("parallel",)),
        name="proj",
    )(x, norm_g.reshape(1, -1), w_bf, bias.reshape(1, -1), cos_t, sin_t)


def _lambda_full(lq1_ref, lk1_ref, lq2_ref, lk2_ref):
    s1 = jnp.sum(lq1_ref[...] * lk1_ref[...], axis=1, keepdims=True)
    s2 = jnp.sum(lq2_ref[...] * lk2_ref[...], axis=1, keepdims=True)
    return jnp.exp(s1) - jnp.exp(s2) + LAMBDA_INIT_A


def _attn_a_body(qi_ref, kj_ref, q_ref, k_ref, v_ref, lq1_ref, lk1_ref, lq2_ref, lk2_ref,
                 o_ref, qs_ref, m_ref, acc_ref, *, tq):
    step = pl.program_id(1)
    qi = qi_ref[step]
    kj = kj_ref[step]
    per_head = 2 * G_A

    @pl.when(kj == 0)
    def _init():
        lane = lax.broadcasted_iota(jnp.int32, (tq, LANES), 1)
        for h in range(KV_A):
            for g in range(G_A):
                slab = q_ref[0, :, (h * G_A + g) * LANES:(h * G_A + g + 1) * LANES]
                zero = jnp.zeros_like(slab)
                qs_ref[h * per_head + g] = jnp.where(lane < HD_A, slab, zero)
                qs_ref[h * per_head + G_A + g] = jnp.where(lane >= HD_A, slab, zero)
        m_ref[...] = jnp.full(m_ref.shape, -jnp.inf, F32)
        acc_ref[...] = jnp.zeros(acc_ref.shape, F32)

    def update(masked):
        ones = jnp.ones((tq, LANES), BF16)
        for h in range(KV_A):
            k = k_ref[0, :, h * LANES:(h + 1) * LANES]
            v_ones = jnp.concatenate([v_ref[0, :, h * VD_A:(h + 1) * VD_A], ones], axis=1)
            for i in range(h * per_head, (h + 1) * per_head):
                s = lax.dot_general(qs_ref[i], k, NT_DIMS, preferred_element_type=F32)
                if masked:
                    row = lax.broadcasted_iota(jnp.int32, s.shape, 0)
                    col = lax.broadcasted_iota(jnp.int32, s.shape, 1)
                    s = jnp.where(col <= row, s, -jnp.inf)
                m_prev = m_ref[i]
                m_new = jnp.maximum(m_prev, jnp.max(s, axis=1, keepdims=True))
                alpha = jnp.exp2(m_prev - m_new)
                p = jnp.exp2(s - _lane_tile(m_new, tq // LANES)).astype(BF16)
                acc_ref[i] = (_lane_tile(alpha, 2) * acc_ref[i]
                              + jnp.dot(p, v_ones, preferred_element_type=F32))
                m_ref[i] = m_new

    @pl.when(kj < qi)
    def _off_diagonal():
        update(False)

    @pl.when(kj == qi)
    def _diagonal():
        update(True)
        lam = _lambda_full(lq1_ref, lk1_ref, lq2_ref, lk2_ref)
        for h in range(KV_A):
            for g in range(G_A):
                a1 = acc_ref[h * per_head + g]
                a2 = acc_ref[h * per_head + G_A + g]
                o1 = a1[:, :VD_A] / a1[:, VD_A:]
                o2 = a2[:, :VD_A] / a2[:, VD_A:]
                o_ref[0, :, (h * G_A + g) * LANES:(h * G_A + g + 1) * LANES] = (
                    o1 - lam * o2).astype(o_ref.dtype)


def _attn_a_prompt(q, kb, vb, lams, batch, seq):
    tq = TQ_A
    nq = seq // tq
    pairs = [(i, j) for i in range(nq) for j in range(i + 1)]
    qi_tab = jnp.asarray([p[0] for p in pairs], jnp.int32)
    kj_tab = jnp.asarray([p[1] for p in pairs], jnp.int32)
    q3 = q.reshape(batch, seq, Q_W_A)
    k3 = kb.reshape(batch, seq, K_W_A)
    v3 = vb.reshape(batch, seq, V_W_A)
    n_slab = KV_A * 2 * G_A
    q_rows = lambda b, s, qi, kj: (b, qi[s], 0)
    k_rows = lambda b, s, qi, kj: (b, kj[s], 0)
    lam_spec = pl.BlockSpec((1, HD_A), lambda b, s, qi, kj: (0, 0))
    grid_spec = pltpu.PrefetchScalarGridSpec(
        num_scalar_prefetch=2,
        grid=(batch, len(pairs)),
        in_specs=[
            pl.BlockSpec((1, tq, Q_W_A), q_rows),
            pl.BlockSpec((1, tq, K_W_A), k_rows),
            pl.BlockSpec((1, tq, V_W_A), k_rows),
            lam_spec, lam_spec, lam_spec, lam_spec,
        ],
        out_specs=pl.BlockSpec((1, tq, H_A * VD_A), q_rows),
        scratch_shapes=[
            pltpu.VMEM((n_slab, tq, LANES), BF16),
            pltpu.VMEM((n_slab, tq, LANES), F32),
            pltpu.VMEM((n_slab, tq, 2 * VD_A), F32),
        ],
    )
    return pl.pallas_call(
        functools.partial(_attn_a_body, tq=tq),
        grid_spec=grid_spec,
        out_shape=jax.ShapeDtypeStruct((batch, seq, H_A * VD_A), BF16),
        compiler_params=_cparams(("parallel", "arbitrary")),
        name="attn_a_prompt",
    )(qi_tab, kj_tab, q3, k3, v3, *lams)


def _attn_a_sample_body(pt_ref, qbd_ref, k_hbm, v_hbm, knew_ref, vnew_ref,
                        lq1_ref, lk1_ref, lq2_ref, lk2_ref, o_ref,
                        kbuf, vbuf, sem, m_ref, l_ref, acc_ref, *, n_pg, t_new):
    c = pl.program_id(1)
    n_c = pl.num_programs(1)
    step = pl.program_id(0) * n_c + c
    n_steps = pl.num_programs(0) * n_c
    slot = step & 1
    qbd = qbd_ref[0]
    n_row = qbd.shape[0]
    h_rows = n_row // KV_A

    def page_copies(page_of, sl):
        cps = []
        for i in range(n_pg):
            page = page_of(i)
            cps.append(pltpu.make_async_copy(k_hbm.at[page], kbuf.at[sl, i], sem.at[sl]))
            cps.append(pltpu.make_async_copy(v_hbm.at[page], vbuf.at[sl, i], sem.at[sl]))
        return cps

    @pl.when(step == 0)
    def _first_fetch():
        for cp in page_copies(lambda i: pt_ref[i], 0):
            cp.start()

    @pl.when(step + 1 < n_steps)
    def _prefetch_next():
        for cp in page_copies(lambda i: pt_ref[(step + 1) * n_pg + i], 1 - slot):
            cp.start()

    for cp in page_copies(lambda i: 0, slot):
        cp.wait()

    @pl.when(c == 0)
    def _init():
        m_ref[...] = jnp.full(m_ref.shape, -jnp.inf, F32)
        l_ref[...] = jnp.zeros(l_ref.shape, F32)
        acc_ref[...] = jnp.zeros(acc_ref.shape, F32)

    def online_update(s, v_of):
        m_prev = m_ref[...]
        m_new = jnp.maximum(m_prev, jnp.max(s, axis=1, keepdims=True))
        alpha = jnp.exp2(m_prev - m_new)
        p = jnp.exp2(s - _lane_tile(m_new, s.shape[1] // LANES))
        l_ref[...] = alpha * l_ref[...] + jnp.sum(p, axis=1, keepdims=True)
        p = p.astype(BF16)
        pvs = []
        for h in range(KV_A):
            pv = jnp.zeros((h_rows, VD_A), F32)
            for i in range(s.shape[1] // LANES):
                pv = pv + jnp.dot(p[h * h_rows:(h + 1) * h_rows, i * LANES:(i + 1) * LANES],
                                  v_of(i, h), preferred_element_type=F32)
            pvs.append(pv)
        acc_ref[...] = alpha * acc_ref[...] + jnp.concatenate(pvs, axis=0)
        m_ref[...] = m_new

    s = jnp.concatenate(
        [jnp.dot(qbd, kbuf[slot, i].astype(BF16), preferred_element_type=F32)
         for i in range(n_pg)], axis=1)
    online_update(
        s, lambda i, h: vbuf[slot, i, pl.ds(h, PAGE_SIZE, stride=KV_A), :].astype(BF16))

    @pl.when(c == n_c - 1)
    def _finish():
        pad = jnp.zeros((PAGE_SIZE - knew_ref.shape[1], knew_ref.shape[2]), F32)
        kn = jnp.concatenate([knew_ref[0], pad], axis=0).astype(BF16)
        vn = jnp.concatenate([vnew_ref[0], pad], axis=0).astype(BF16)
        sn = lax.dot_general(qbd, kn, NT_DIMS, preferred_element_type=F32)
        t = lax.broadcasted_iota(jnp.int32, sn.shape, 0) & (t_new - 1)
        col = lax.broadcasted_iota(jnp.int32, sn.shape, 1)
        online_update(jnp.where(col <= t, sn, -jnp.inf),
                      lambda i, h: vn[:, h * VD_A:(h + 1) * VD_A])
        lam = _lambda_full(lq1_ref, lk1_ref, lq2_ref, lk2_ref)
        o = acc_ref[...] / l_ref[...]
        half = h_rows // 2
        o_ref[0] = jnp.concatenate(
            [o[h * h_rows:h * h_rows + half] - lam * o[h * h_rows + half:(h + 1) * h_rows]
             for h in range(KV_A)], axis=0)


def _attn_a_sample(page_table, qbd, cache_k, cache_v, knew, vnew, lams):
    n_seq, n_pages = page_table.shape
    n_pool = cache_k.shape[0]
    n_pg = PAGES_PER_STEP
    n_row = qbd.shape[1]
    t_new = n_row // (2 * KV_A * G_A)
    ck = jnp.transpose(cache_k, (0, 2, 3, 4, 1)).reshape(n_pool, K_W_A, PAGE_SIZE)
    cv = cache_v.reshape(n_pool, PAGE_SIZE * KV_A, VD_A)
    per_seq = lambda s, c, pt: (s, 0, 0)
    lam_spec = pl.BlockSpec((1, HD_A), lambda s, c, pt: (0, 0))
    hbm = pl.BlockSpec(memory_space=pl.ANY)
    grid_spec = pltpu.PrefetchScalarGridSpec(
        num_scalar_prefetch=1,
        grid=(n_seq, n_pages // n_pg),
        in_specs=[pl.BlockSpec((1, n_row, K_W_A), per_seq), hbm, hbm,
                  pl.BlockSpec((1,) + knew.shape[1:], per_seq),
                  pl.BlockSpec((1,) + vnew.shape[1:], per_seq),
                  lam_spec, lam_spec, lam_spec, lam_spec],
        out_specs=pl.BlockSpec((1, n_row // 2, VD_A), per_seq),
        scratch_shapes=[
            pltpu.VMEM((2, n_pg, K_W_A, PAGE_SIZE), F32),
            pltpu.VMEM((2, n_pg, PAGE_SIZE * KV_A, VD_A), F32),
            pltpu.SemaphoreType.DMA((2,)),
            pltpu.VMEM((n_row, LANES), F32),
            pltpu.VMEM((n_row, LANES), F32),
            pltpu.VMEM((n_row, VD_A), F32),
        ],
    )
    return pl.pallas_call(
        functools.partial(_attn_a_sample_body, n_pg=n_pg, t_new=t_new),
        grid_spec=grid_spec,
        out_shape=jax.ShapeDtypeStruct((n_seq, n_row // 2, VD_A), F32),
        compiler_params=_cparams(("arbitrary", "arbitrary")),
        name="attn_a_sample",
    )(page_table.reshape(-1), qbd, ck, cv, knew, vnew, *lams)


def _out_a_body(o_ref, gate_ref, x_ref, sg_ref, w_ref, y_ref):
    sg = sg_ref[...]
    us = []
    for j in range(H_A):
        sl = slice(j * VD_A, (j + 1) * VD_A)
        oj = o_ref[:, sl].astype(F32)
        ms = jnp.mean(oj * oj, axis=-1, keepdims=True)
        yj = ((oj * lax.rsqrt(ms + EPS)) * sg) * (1.0 - LAMBDA_INIT_A)
        gj = gate_ref[:, sl].astype(F32)
        us.append((yj * (gj * jax.nn.sigmoid(gj))).astype(BF16))
    u = jnp.concatenate(us, axis=1)
    y_ref[...] = x_ref[...] + jnp.dot(u, w_ref[...], preferred_element_type=F32)


def _out_a(o, gate, x, subln_g, w_bf):
    n = x.shape[0]
    tm = min(TM, n)
    row = lambda i: (i, 0)
    fixed = lambda i: (0, 0)
    return pl.pallas_call(
        _out_a_body,
        grid=(n // tm,),
        in_specs=[
            pl.BlockSpec((tm, D_MODEL), row),
            pl.BlockSpec((tm, D_MODEL), row),
            pl.BlockSpec((tm, D_MODEL), row),
            pl.BlockSpec((1, VD_A), fixed),
            pl.BlockSpec((D_MODEL, D_MODEL), fixed),
        ],
        out_specs=pl.BlockSpec((tm, D_MODEL), row),
        out_shape=jax.ShapeDtypeStruct((n, D_MODEL), F32),
        compiler_params=_cparams(("parallel",)),
        name="out_a",
    )(o, gate, x, subln_g.reshape(1, -1), w_bf)


def _sink_softmax(s, sink_rows):
    sk = sink_rows * LOG2E
    m = jnp.maximum(jnp.max(s, axis=1, keepdims=True), sk)
    e = jnp.exp2(s - _lane_tile(m, s.shape[1] // LANES))
    den = jnp.sum(e, axis=1, keepdims=True) + jnp.exp2(sk - m)
    return e / _lane_tile(den, s.shape[1] // LANES)


def _stack_heads(slabs, n_t):
    lane_head = lax.broadcasted_iota(jnp.int32, (n_t, K_W_B), 1) >> (HD_B.bit_length() - 1)
    return jnp.concatenate(
        [jnp.where(lane_head == h, slabs[g], jnp.zeros_like(slabs[g]))
         for h in range(KV_B) for g in range(len(slabs))], axis=0)


def _unstack_heads(pv, n_g, n_t):
    lane_head = lax.broadcasted_iota(jnp.int32, (n_t, K_W_B), 1) >> (HD_B.bit_length() - 1)
    outs = []
    for g in range(n_g):
        acc = jnp.zeros((n_t, K_W_B), F32)
        for h in range(KV_B):
            r0 = (h * n_g + g) * n_t
            acc = jnp.where(lane_head == h, pv[r0:r0 + n_t], acc)
        outs.append(acc)
    return outs


def _attn_b_body(q_ref, kc_ref, kp_ref, vc_ref, vp_ref, sink_ref, o_ref, *, tq):
    first_tile = pl.program_id(1) == 0
    w = WINDOW
    head_rows = G_B * w
    sink_rows = sink_ref[...]
    t = lax.broadcasted_iota(jnp.int32, (w, 2 * w), 0)
    col = lax.broadcasted_iota(jnp.int32, (w, 2 * w), 1)
    in_window = (col > t) & (col <= t + w)
    bias = jnp.where(in_window, 0.0, -jnp.inf)
    bias_no_prev = jnp.where(in_window & (col >= w), 0.0, -jnp.inf)
    lane_head = lax.broadcasted_iota(jnp.int32, (2 * w, V_W_B), 1) >> (HD_B.bit_length() - 1)
    for j in range(tq // w):
        if j == 0:
            k2 = jnp.concatenate([kp_ref[0], kc_ref[0, :w]], axis=0)
            v2 = jnp.concatenate([vp_ref[0], vc_ref[0, :w]], axis=0)
            b = jnp.where(first_tile, bias_no_prev, bias)
        else:
            k2 = kc_ref[0, (j - 1) * w:(j + 1) * w]
            v2 = vc_ref[0, (j - 1) * w:(j + 1) * w]
            b = bias
        qsub = q_ref[0, j * w:(j + 1) * w, :]
        qst = _stack_heads([qsub[:, g * K_W_B:(g + 1) * K_W_B] for g in range(G_B)], w)
        s = lax.dot_general(qst, k2, NT_DIMS, preferred_element_type=F32)
        p = _sink_softmax(s + jnp.tile(b, (H_B, 1)), sink_rows).astype(BF16)
        p_wide = jnp.concatenate(
            [p[h * head_rows:(h + 1) * head_rows] for h in range(KV_B)], axis=1)
        v_heads = jnp.concatenate(
            [jnp.where(lane_head == h, v2, jnp.zeros_like(v2)) for h in range(KV_B)], axis=0)
        o = jnp.dot(p_wide, v_heads, preferred_element_type=F32)
        for g in range(G_B):
            o_ref[0, j * w:(j + 1) * w, g * K_W_B:(g + 1) * K_W_B] = (
                o[g * w:(g + 1) * w].astype(o_ref.dtype))


def _attn_b_prompt(q, kb, vb, sink_rows, batch, seq):
    tq = TQ_B
    r = tq // WINDOW
    q3 = q.reshape(batch, seq, Q_W_B)
    k3 = kb.reshape(batch, seq, K_W_B)
    v3 = vb.reshape(batch, seq, V_W_B)
    cur = lambda b, i: (b, i, 0)
    prev = lambda b, i: (b, jnp.maximum(i * r - 1, 0), 0)
    return pl.pallas_call(
        functools.partial(_attn_b_body, tq=tq),
        grid=(batch, seq // tq),
        in_specs=[
            pl.BlockSpec((1, tq, Q_W_B), cur),
            pl.BlockSpec((1, tq, K_W_B), cur),
            pl.BlockSpec((1, WINDOW, K_W_B), prev),
            pl.BlockSpec((1, tq, V_W_B), cur),
            pl.BlockSpec((1, WINDOW, V_W_B), prev),
            pl.BlockSpec(sink_rows.shape, lambda b, i: (0, 0)),
        ],
        out_specs=pl.BlockSpec((1, tq, Q_W_B), cur),
        out_shape=jax.ShapeDtypeStruct((batch, seq, Q_W_B), BF16),
        compiler_params=_cparams(("parallel", "arbitrary")),
        name="attn_b_prompt",
    )(q3, k3, k3, v3, v3, sink_rows)


def _attn_b_sample_body(qbd_ref, kt_ref, vt_ref, knew_ref, vnew_ref, sink_ref,
                        o_ref, kto_ref, vto_ref, *, t_new):
    w = WINDOW
    n_row = qbd_ref.shape[1]
    lane = lax.broadcasted_iota(jnp.int32, (K_W_B, w), 1)
    t = lax.broadcasted_iota(jnp.int32, (n_row, 2 * w), 0) & (t_new - 1)
    col = lax.broadcasted_iota(jnp.int32, (n_row, 2 * w), 1)
    valid = ((col > t) & (col < w)) | ((col >= 2 * w - t_new) & (col <= t + 2 * w - t_new))
    sink_rows = sink_ref[...]

    def new_tile(new_rows):
        pad = jnp.zeros((w - new_rows.shape[0], new_rows.shape[1]), F32)
        return jnp.concatenate([pad, new_rows], axis=0).T

    def shifted(buf_t, new_t):
        return jnp.where(lane < w - t_new, pltpu.roll(buf_t, w - t_new, axis=1), new_t)

    for b in range(qbd_ref.shape[0]):
        kt, vt = kt_ref[b], vt_ref[b]
        knew_t, vnew_t = new_tile(knew_ref[b]), new_tile(vnew_ref[b])
        kto_ref[b] = shifted(kt, knew_t)
        vto_ref[b] = shifted(vt, vnew_t)
        k2t = jnp.concatenate([kt, knew_t], axis=1).astype(BF16)
        v2 = jnp.concatenate([vt, vnew_t], axis=1).T.astype(BF16)
        s = jnp.dot(qbd_ref[b], k2t, preferred_element_type=F32)
        p = _sink_softmax(jnp.where(valid, s, -jnp.inf), sink_rows)
        pv = jnp.dot(p.astype(BF16), v2, preferred_element_type=F32)
        o_ref[b] = _unstack_heads(pv, 1, n_row // KV_B)[0]


def _attn_b_sample(qbd, kbuf_t, vbuf_t, knew, vnew, sink_rows, t_new):
    n_seq, n_row, _ = qbd.shape
    nb = math.gcd(n_seq, SEQS_PER_STEP_B)
    per_seq = lambda s: (s, 0, 0)
    buf_spec = pl.BlockSpec((nb, K_W_B, WINDOW), per_seq)
    buf_shape = jax.ShapeDtypeStruct((n_seq, K_W_B, WINDOW), F32)
    return pl.pallas_call(
        functools.partial(_attn_b_sample_body, t_new=t_new),
        grid=(n_seq // nb,),
        in_specs=[
            pl.BlockSpec((nb, n_row, K_W_B), per_seq),
            buf_spec,
            buf_spec,
            pl.BlockSpec((nb,) + knew.shape[1:], per_seq),
            pl.BlockSpec((nb,) + vnew.shape[1:], per_seq),
            pl.BlockSpec(sink_rows.shape, lambda s: (0, 0)),
        ],
        out_specs=[pl.BlockSpec((nb, n_row // KV_B, K_W_B), per_seq), buf_spec, buf_spec],
        out_shape=[jax.ShapeDtypeStruct((n_seq, n_row // KV_B, K_W_B), F32), buf_shape, buf_shape],
        compiler_params=_cparams(("parallel",)),
        name="attn_b_sample",
    )(qbd, kbuf_t, vbuf_t, knew, vnew, sink_rows)


def _out_b_body(o_ref, gate_ref, x_ref, w_ref, nf_ref, y_ref):
    g = gate_ref[...].astype(F32)
    u = (o_ref[...].astype(F32) * (g * jax.nn.sigmoid(g))).astype(BF16)
    y = x_ref[...] + jnp.dot(u, w_ref[...], preferred_element_type=F32)
    ms = jnp.mean(y * y, axis=-1, keepdims=True)
    y_ref[...] = (y * lax.rsqrt(ms + EPS)) * nf_ref[...]


def _out_b(o, gate, x, w_bf, norm_f):
    n = x.shape[0]
    tm = min(TM, n)
    row = lambda i: (i, 0)
    fixed = lambda i: (0, 0)
    return pl.pallas_call(
        _out_b_body,
        grid=(n // tm,),
        in_specs=[
            pl.BlockSpec((tm, D_MODEL), row),
            pl.BlockSpec((tm, D_MODEL), row),
            pl.BlockSpec((tm, D_MODEL), row),
            pl.BlockSpec((D_MODEL, D_MODEL), fixed),
            pl.BlockSpec((1, D_MODEL), fixed),
        ],
        out_specs=pl.BlockSpec((tm, D_MODEL), row),
        out_shape=jax.ShapeDtypeStruct((n, D_MODEL), F32),
        compiler_params=_cparams(("parallel",)),
        name="out_b",
    )(o, gate, x, w_bf, norm_f.reshape(1, -1))


def _pad_rows(x, rows):
    return jnp.pad(x, ((0, 0), (0, rows - x.shape[1]), (0, 0)))


def kernel(x_prompt, x_sample, cache_a_k, cache_a_v, page_table, state_b_k, state_b_v,
           norm_a, w_in_a, lambda_q1, lambda_k1, lambda_q2, lambda_k2, subln_a, w_out_a,
           norm_b, w_in_b, b_in_b, sinks_b, w_out_b, norm_f):
    batch, seq, _ = x_prompt.shape
    n_seq, t_new, _ = x_sample.shape
    past = page_table.shape[1] * PAGE_SIZE
    xp = x_prompt.reshape(batch * seq, D_MODEL)
    xs = x_sample.reshape(n_seq * t_new, D_MODEL)

    cos_p, sin_p = _rope_tables(jnp.arange(seq))
    cos_s, sin_s = _rope_tables(past + jnp.arange(t_new))
    cos_s = jnp.tile(cos_s, (n_seq, 1))
    sin_s = jnp.tile(sin_s, (n_seq, 1))
    lams = [v.reshape(1, HD_A) for v in (lambda_q1, lambda_k1, lambda_q2, lambda_k2)]

    wa = w_in_a.astype(BF16)
    ba = jnp.zeros((w_in_a.shape[1],), F32)
    woa = w_out_a.astype(BF16)
    dims_a = dict(q_w=Q_W_A, k_w=K_W_A, v_w=V_W_A)
    q_p, kt_p, vh_p, kb_p, vb_p, gate_p = _proj(
        xp, norm_a, wa, ba, cos_p, sin_p, seq, k_layout="transposed", v_layout="head_rows",
        vb_layout="rows", **dims_a)
    q_s, k_s, vh_s, _, _, gate_s = _proj(
        xs, norm_a, wa, ba, cos_s, sin_s, n_seq * t_new, k_layout="rows", v_layout="head_rows",
        vb_layout="rows", **dims_a)

    o_p = _attn_a_prompt(q_p, kb_p, vb_p, lams, batch, seq)
    xp1 = _out_a(o_p.reshape(batch * seq, D_MODEL), gate_p, xp, subln_a, woa)

    qr = q_s.reshape(n_seq, t_new, KV_A, G_A, 2, HD_A).transpose(0, 2, 4, 3, 1, 5)
    qr = qr.reshape(n_seq, 2 * KV_A * G_A * t_new, HD_A)
    blk = jnp.repeat(jnp.arange(2 * KV_A), G_A * t_new)
    onehot = (blk[:, None] == jnp.arange(2 * KV_A)[None, :]).astype(BF16)
    qbd_a = (qr[:, :, None, :] * onehot[None, :, :, None]).reshape(n_seq, -1, K_W_A)
    knew_a = _pad_rows(k_s.reshape(n_seq, t_new, K_W_A), 8)
    vnew_a = _pad_rows(vh_s.reshape(n_seq, t_new, V_W_A), 8)
    o_s = _attn_a_sample(page_table, qbd_a, cache_a_k, cache_a_v, knew_a, vnew_a, lams)
    o_s = o_s.reshape(n_seq, KV_A, G_A, t_new, VD_A).transpose(0, 3, 1, 2, 4)
    xs1 = _out_a(o_s.reshape(n_seq * t_new, D_MODEL), gate_s, xs, subln_a, woa)

    perm = (jnp.arange(G_B)[:, None, None] * HD_B
            + jnp.arange(KV_B)[None, :, None] * (G_B * HD_B)
            + jnp.arange(HD_B)[None, None, :]).reshape(-1)
    gate0 = Q_W_B + K_W_B + V_W_B
    cols = jnp.concatenate([perm, jnp.arange(Q_W_B, gate0), gate0 + perm])
    wb = w_in_b[:, cols].astype(BF16)
    bb = b_in_b[cols]
    wob = w_out_b[perm, :].astype(BF16)
    dims_b = dict(q_w=Q_W_B, k_w=K_W_B, v_w=V_W_B)
    qb_p, kt_bp, vt_bp, kbb_p, vbb_p, gb_p = _proj(
        xp1, norm_b, wb, bb, cos_p, sin_p, seq, k_layout="transposed", v_layout="transposed",
        vb_layout="rows", **dims_b)
    qb_s, kf_s, vf_s, _, _, gb_s = _proj(
        xs1, norm_b, wb, bb, cos_s, sin_s, n_seq * t_new, k_layout="rows", v_layout="rows",
        vb_layout="rows", **dims_b)

    sink_p = jnp.broadcast_to(jnp.repeat(sinks_b, WINDOW)[:, None], (H_B * WINDOW, LANES))
    ob_p = _attn_b_prompt(qb_p, kbb_p, vbb_p, sink_p, batch, seq)
    y_p = _out_b(ob_p.reshape(batch * seq, D_MODEL), gb_p, xp1, wob, norm_f)

    qr = qb_s.reshape(n_seq, t_new, G_B, KV_B, HD_B).transpose(0, 3, 2, 1, 4)
    qr = qr.reshape(n_seq, KV_B * G_B * t_new, HD_B)
    blk = jnp.repeat(jnp.arange(KV_B), G_B * t_new)
    onehot = (blk[:, None] == jnp.arange(KV_B)[None, :]).astype(BF16)
    qbd_b = (qr[:, :, None, :] * onehot[None, :, :, None]).reshape(n_seq, -1, K_W_B)
    sink_s = jnp.broadcast_to(jnp.repeat(sinks_b, t_new)[:, None], (H_B * t_new, LANES))
    front = ((0, 0), (8 - t_new, 0), (0, 0))
    knew_b = jnp.pad(kf_s.reshape(n_seq, t_new, K_W_B), front)
    vnew_b = jnp.pad(vf_s.reshape(n_seq, t_new, V_W_B), front)
    to_minor = lambda buf: buf.transpose(0, 2, 3, 1).reshape(buf.shape[0], -1, WINDOW)
    from_minor = lambda t: t.reshape(t.shape[0], KV_B, HD_B, WINDOW).transpose(0, 3, 1, 2)
    ob_s, kt_bs, vt_bs = _attn_b_sample(qbd_b, to_minor(state_b_k), to_minor(state_b_v),
                                        knew_b, vnew_b, sink_s, t_new)
    ob_s = ob_s.reshape(n_seq, G_B, t_new, K_W_B).transpose(0, 2, 1, 3)
    y_s = _out_b(ob_s.reshape(n_seq * t_new, D_MODEL), gb_s, xs1, wob, norm_f)

    return (
        y_p.reshape(batch, seq, D_MODEL),
        y_s.reshape(n_seq, t_new, D_MODEL),
        kt_p.reshape(batch, KV_A, 2, HD_A, seq).transpose(0, 4, 1, 2, 3),
        vh_p.reshape(batch, seq, KV_A, VD_A),
        k_s.reshape(n_seq, t_new, KV_A, 2, HD_A),
        vh_s.reshape(n_seq, t_new, KV_A, VD_A),
        from_minor(kt_bp[:, :, seq - WINDOW:]),
        from_minor(vt_bp[:, :, seq - WINDOW:]),
        from_minor(kt_bs),
        from_minor(vt_bs),
    )
```

```python
import functools
import math

import jax
import jax.numpy as jnp
from jax import lax
from jax.experimental import pallas as pl
from jax.experimental.pallas import tpu as pltpu

F32 = jnp.float32
BF16 = jnp.bfloat16

D_MODEL = 1024
PAGE_SIZE = 128
EPS = 1e-5
ROPE_THETA = 500000.0
H_A, HD_A, KV_A = 8, 64, 2
VD_A = 2 * HD_A
G_A = H_A // KV_A
Q_W_A = H_A * 2 * HD_A
K_W_A = KV_A * 2 * HD_A
V_W_A = KV_A * VD_A
LAMBDA_INIT_A = 0.8 - 0.6 * math.exp(-0.3 * 0)
H_B, HD_B, KV_B = 16, 64, 4
G_B = H_B // KV_B
WINDOW = 128
Q_W_B = H_B * HD_B
K_W_B = KV_B * HD_B
V_W_B = KV_B * HD_B
ROT = HD_A // 4
HALF = ROT // 2

LOG2E = math.log2(math.e)
Q_SCALE = (HD_A ** -0.5) * LOG2E

LANES = 128
VMEM_LIMIT = 48 * 1024 * 1024

TM = 512
TQ_A = 512
TQ_B = 512
PAGES_PER_STEP = 32
SEQS_PER_STEP_B = 8

NT_DIMS = (((1,), (1,)), ((), ()))


def _lane_tile(x, n):
    return x if n == 1 else jnp.concatenate([x] * n, axis=1)


def _cparams(sem):
    return pltpu.CompilerParams(dimension_semantics=sem, vmem_limit_bytes=VMEM_LIMIT)


def _rope_tables(pos):
    inv = jnp.power(ROPE_THETA, -jnp.arange(HALF, dtype=F32) * 2.0 / ROT)
    ang = pos.astype(F32)[:, None] * inv[None, :]
    cos, sin = jnp.cos(ang), jnp.sin(ang)
    ones = jnp.ones((pos.shape[0], HD_A - ROT), F32)
    cos_h = jnp.concatenate([cos, cos, ones], axis=1)
    sin_h = jnp.concatenate([-sin, sin, 0.0 * ones], axis=1)
    return jnp.tile(cos_h, (1, LANES // HD_A)), jnp.tile(sin_h, (1, LANES // HD_A))


def _proj_body(x_ref, g_ref, w_ref, b_ref, cos_ref, sin_ref,
               q_ref, k_ref, v_ref, kb_ref, vb_ref, gate_ref, *, q_w, k_w, v_w,
               k_layout, v_layout, vb_layout):
    x = x_ref[...]
    ms = jnp.mean(x * x, axis=-1, keepdims=True)
    xn = (x * lax.rsqrt(ms + EPS)) * g_ref[...]
    z = jnp.dot(xn.astype(BF16), w_ref[...], preferred_element_type=F32) + b_ref[...]
    cos = cos_ref[...]
    sin = sin_ref[...]
    lane = lax.broadcasted_iota(jnp.int32, cos.shape, 1)
    first_half = (lane & (HD_A - 1)) < HALF

    def rope(blk):
        partner = jnp.where(first_half, pltpu.roll(blk, LANES - HALF, axis=1),
                            pltpu.roll(blk, HALF, axis=1))
        return blk * cos + partner * sin

    for j in range(q_w // LANES):
        sl = slice(j * LANES, (j + 1) * LANES)
        q_ref[:, sl] = (rope(z[:, sl]) * Q_SCALE).astype(BF16)
    kr = jnp.concatenate(
        [rope(z[:, q_w + j * LANES:q_w + (j + 1) * LANES]) for j in range(k_w // LANES)], axis=1)
    v = z[:, q_w + k_w:q_w + k_w + v_w]
    kb_ref[...] = kr.astype(BF16)
    gate_ref[...] = z[:, q_w + k_w + v_w:].astype(gate_ref.dtype)
    tm = kr.shape[0]
    if k_layout == "transposed":
        k_ref[0] = kr.T
    else:
        k_ref[...] = kr
    if "transposed" in (v_layout, vb_layout):
        vt = v.T
    if vb_layout == "transposed":
        vb_ref[0] = vt.astype(BF16)
    else:
        vb_ref[...] = v.astype(BF16)
    if v_layout == "transposed":
        v_ref[0] = vt
    elif v_layout == "head_rows":
        n_h = v_w // LANES
        for h in range(n_h):
            v_ref[pl.ds(h, tm, stride=n_h), :] = v[:, h * LANES:(h + 1) * LANES]
    else:
        v_ref[...] = v


def _proj(x, norm_g, w_bf, bias, cos_t, sin_t, seq, *, q_w, k_w, v_w,
          k_layout, v_layout, vb_layout):
    n = x.shape[0]
    tm = min(TM, n, seq)
    width = w_bf.shape[1]
    gate_w = width - q_w - k_w - v_w
    n_tab = cos_t.shape[0] // tm
    per_seq = seq // tm
    row = lambda i: (i, 0)
    fixed = lambda i: (0, 0)
    tab = lambda i: (i % n_tab, 0)

    def out(w, layout, dtype):
        if layout == "transposed":
            return (pl.BlockSpec((1, w, tm), lambda i: (i // per_seq, 0, i % per_seq)),
                    jax.ShapeDtypeStruct((n // seq, w, seq), dtype))
        if layout == "head_rows":
            n_h = w // LANES
            return (pl.BlockSpec((tm * n_h, LANES), row),
                    jax.ShapeDtypeStruct((n * n_h, LANES), dtype))
        return pl.BlockSpec((tm, w), row), jax.ShapeDtypeStruct((n, w), dtype)

    k_spec, k_shape = out(k_w, k_layout, F32)
    v_spec, v_shape = out(v_w, v_layout, F32)
    vb_spec, vb_shape = out(v_w, vb_layout, BF16)
    return pl.pallas_call(
        functools.partial(_proj_body, q_w=q_w, k_w=k_w, v_w=v_w,
                          k_layout=k_layout, v_layout=v_layout, vb_layout=vb_layout),
        grid=(n // tm,),
        in_specs=[
            pl.BlockSpec((tm, D_MODEL), row),
            pl.BlockSpec((1, D_MODEL), fixed),
            pl.BlockSpec((D_MODEL, width), fixed),
            pl.BlockSpec((1, width), fixed),
            pl.BlockSpec((tm, LANES), tab),
            pl.BlockSpec((tm, LANES), tab),
        ],
        out_specs=[
            pl.BlockSpec((tm, q_w), row),
            k_spec,
            v_spec,
            pl.BlockSpec((tm, k_w), row),
            vb_spec,
            pl.BlockSpec((tm, gate_w), row),
        ],
        out_shape=[
            jax.ShapeDtypeStruct((n, q_w), BF16),
            k_shape,
            v_shape,
            jax.ShapeDtypeStruct((n, k_w), BF16),
            vb_shape,
            jax.ShapeDtypeStruct((n, gate_w), BF16),
        ],
        compiler_params=_cparams(("parallel",)),
        name="proj",
    )(x, norm_g.reshape(1, -1), w_bf, bias.reshape(1, -1), cos_t, sin_t)


def _lambda_full(lq1_ref, lk1_ref, lq2_ref, lk2_ref):
    s1 = jnp.sum(lq1_ref[...] * lk1_ref[...], axis=1, keepdims=True)
    s2 = jnp.sum(lq2_ref[...] * lk2_ref[...], axis=1, keepdims=True)
    return jnp.exp(s1) - jnp.exp(s2) + LAMBDA_INIT_A


def _attn_a_body(b_ref, qi_ref, kj_ref, pt_ref,
                 q_ref, k_ref, v_ref, qbd_ref, k_hbm, v_hbm, knew_ref, vnew_ref,
                 lq1_ref, lk1_ref, lq2_ref, lk2_ref,
                 o_ref, os_ref,
                 qs_ref, m_ref, acc_ref, kbuf, vbuf, sem, ms_ref, ls_ref, accs_ref,
                 *, tq, n_flash, n_paged, n_pg, n_c, t_new):
    step = pl.program_id(0)
    qi = qi_ref[step]
    kj = kj_ref[step]
    flash_on = step < n_flash
    paged_on = step < n_paged
    c = lax.rem(step, n_c)
    slot = step & 1
    per_head = 2 * G_A
    qbd = qbd_ref[0]
    n_row = qbd.shape[0]
    h_rows = n_row // KV_A

    def page_copies(page_of, sl):
        cps = []
        for i in range(n_pg):
            page = page_of(i)
            cps.append(pltpu.make_async_copy(k_hbm.at[page], kbuf.at[sl, i], sem.at[sl]))
            cps.append(pltpu.make_async_copy(v_hbm.at[page], vbuf.at[sl, i], sem.at[sl]))
        return cps

    @pl.when(step == 0)
    def _first_fetch():
        for cp in page_copies(lambda i: pt_ref[i], 0):
            cp.start()

    @pl.when(step + 1 < n_paged)
    def _prefetch_next():
        for cp in page_copies(lambda i: pt_ref[(step + 1) * n_pg + i], 1 - slot):
            cp.start()

    @pl.when(paged_on)
    def _wait_pages():
        for cp in page_copies(lambda i: 0, slot):
            cp.wait()

    @pl.when(flash_on & (kj == 0))
    def _init_flash():
        lane = lax.broadcasted_iota(jnp.int32, (tq, LANES), 1)
        for h in range(KV_A):
            for g in range(G_A):
                slab = q_ref[0, :, (h * G_A + g) * LANES:(h * G_A + g + 1) * LANES]
                zero = jnp.zeros_like(slab)
                qs_ref[h * per_head + g] = jnp.where(lane < HD_A, slab, zero)
                qs_ref[h * per_head + G_A + g] = jnp.where(lane >= HD_A, slab, zero)
        m_ref[...] = jnp.full(m_ref.shape, -jnp.inf, F32)
        acc_ref[...] = jnp.zeros(acc_ref.shape, F32)

    @pl.when(paged_on & (c == 0))
    def _init_paged():
        ms_ref[...] = jnp.full(ms_ref.shape, -jnp.inf, F32)
        ls_ref[...] = jnp.zeros(ls_ref.shape, F32)
        accs_ref[...] = jnp.zeros(accs_ref.shape, F32)

    def flash_update(masked):
        ones = jnp.ones((tq, LANES), BF16)
        for h in range(KV_A):
            k = k_ref[0, :, h * LANES:(h + 1) * LANES]
            v_ones = jnp.concatenate([v_ref[0, :, h * VD_A:(h + 1) * VD_A], ones], axis=1)
            for i in range(h * per_head, (h + 1) * per_head):
                s = lax.dot_general(qs_ref[i], k, NT_DIMS, preferred_element_type=F32)
                if masked:
                    row = lax.broadcasted_iota(jnp.int32, s.shape, 0)
                    col = lax.broadcasted_iota(jnp.int32, s.shape, 1)
                    s = jnp.where(col <= row, s, -jnp.inf)
                m_prev = m_ref[i]
                m_new = jnp.maximum(m_prev, jnp.max(s, axis=1, keepdims=True))
                alpha = jnp.exp2(m_prev - m_new)
                p = jnp.exp2(s - _lane_tile(m_new, tq // LANES)).astype(BF16)
                acc_ref[i] = (_lane_tile(alpha, 2) * acc_ref[i]
                              + jnp.dot(p, v_ones, preferred_element_type=F32))
                m_ref[i] = m_new

    def flash_finish():
        lam = _lambda_full(lq1_ref, lk1_ref, lq2_ref, lk2_ref)
        for h in range(KV_A):
            for g in range(G_A):
                a1 = acc_ref[h * per_head + g]
                a2 = acc_ref[h * per_head + G_A + g]
                o1 = a1[:, :VD_A] / a1[:, VD_A:]
                o2 = a2[:, :VD_A] / a2[:, VD_A:]
                o_ref[0, :, (h * G_A + g) * LANES:(h * G_A + g + 1) * LANES] = (
                    o1 - lam * o2).astype(o_ref.dtype)

    def online_update(s, v_of):
        m_prev = ms_ref[...]
        m_new = jnp.maximum(m_prev, jnp.max(s, axis=1, keepdims=True))
        alpha = jnp.exp2(m_prev - m_new)
        p = jnp.exp2(s - _lane_tile(m_new, s.shape[1] // LANES))
        ls_ref[...] = alpha * ls_ref[...] + jnp.sum(p, axis=1, keepdims=True)
        p = p.astype(BF16)
        pvs = []
        for h in range(KV_A):
            pv = jnp.zeros((h_rows, VD_A), F32)
            for i in range(s.shape[1] // LANES):
                pv = pv + jnp.dot(p[h * h_rows:(h + 1) * h_rows, i * LANES:(i + 1) * LANES],
                                  v_of(i, h), preferred_element_type=F32)
            pvs.append(pv)
        accs_ref[...] = alpha * accs_ref[...] + jnp.concatenate(pvs, axis=0)
        ms_ref[...] = m_new

    def paged_update():
        s = jnp.concatenate(
            [jnp.dot(qbd, kbuf[slot, i].astype(BF16), preferred_element_type=F32)
             for i in range(n_pg)], axis=1)
        online_update(
            jnp.where(paged_on, s, -jnp.inf),
            lambda i, h: vbuf[slot, i, pl.ds(h, PAGE_SIZE, stride=KV_A), :].astype(BF16))

    @pl.when(flash_on & (kj < qi))
    def _off_diagonal():
        flash_update(False)
        paged_update()

    @pl.when(flash_on & (kj == qi))
    def _diagonal():
        flash_update(True)
        flash_finish()
        paged_update()

    @pl.when(jnp.logical_not(flash_on))
    def _sample_only():
        paged_update()

    @pl.when(paged_on & (c == n_c - 1))
    def _paged_finish():
        pad = jnp.zeros((PAGE_SIZE - knew_ref.shape[1], knew_ref.shape[2]), F32)
        kn = jnp.concatenate([knew_ref[0], pad], axis=0).astype(BF16)
        vn = jnp.concatenate([vnew_ref[0], pad], axis=0).astype(BF16)
        sn = lax.dot_general(qbd, kn, NT_DIMS, preferred_element_type=F32)
        t = lax.broadcasted_iota(jnp.int32, sn.shape, 0) & (t_new - 1)
        col = lax.broadcasted_iota(jnp.int32, sn.shape, 1)
        online_update(jnp.where(col <= t, sn, -jnp.inf),
                      lambda i, h: vn[:, h * VD_A:(h + 1) * VD_A])
        lam = _lambda_full(lq1_ref, lk1_ref, lq2_ref, lk2_ref)
        o = accs_ref[...] / ls_ref[...]
        half = h_rows // 2
        os_ref[0] = jnp.concatenate(
            [o[h * h_rows:h * h_rows + half] - lam * o[h * h_rows + half:(h + 1) * h_rows]
             for h in range(KV_A)], axis=0)


def _attn_a(q, kb, vb, batch, seq, page_table, qbd, cache_k, cache_v, knew, vnew, lams):
    tq = TQ_A
    nq = seq // tq
    n_seq, n_pages = page_table.shape
    n_pool = cache_k.shape[0]
    n_pg = PAGES_PER_STEP
    n_c = n_pages // n_pg
    n_row = qbd.shape[1]
    t_new = n_row // (2 * KV_A * G_A)
    pairs = [(b, i, j) for b in range(batch) for i in range(nq) for j in range(i + 1)]
    n_flash, n_paged = len(pairs), n_seq * n_c
    n_steps = max(n_flash, n_paged)
    pairs = pairs + [pairs[-1]] * (n_steps - n_flash)
    b_tab, qi_tab, kj_tab = (jnp.asarray([p[a] for p in pairs], jnp.int32) for a in range(3))
    q3 = q.reshape(batch, seq, Q_W_A)
    k3 = kb.reshape(batch, seq, K_W_A)
    v3 = vb.reshape(batch, seq, V_W_A)
    ck = jnp.transpose(cache_k, (0, 2, 3, 4, 1)).reshape(n_pool, K_W_A, PAGE_SIZE)
    cv = cache_v.reshape(n_pool, PAGE_SIZE * KV_A, VD_A)
    n_slab = KV_A * 2 * G_A
    q_rows = lambda s, b, qi, kj, pt: (b[s], qi[s], 0)
    k_rows = lambda s, b, qi, kj, pt: (b[s], kj[s], 0)
    per_seq = lambda s, b, qi, kj, pt: (jnp.minimum(s // n_c, n_seq - 1), 0, 0)
    lam_spec = pl.BlockSpec((1, HD_A), lambda s, b, qi, kj, pt: (0, 0))
    hbm = pl.BlockSpec(memory_space=pl.ANY)
    grid_spec = pltpu.PrefetchScalarGridSpec(
        num_scalar_prefetch=4,
        grid=(n_steps,),
        in_specs=[
            pl.BlockSpec((1, tq, Q_W_A), q_rows),
            pl.BlockSpec((1, tq, K_W_A), k_rows),
            pl.BlockSpec((1, tq, V_W_A), k_rows),
            pl.BlockSpec((1, n_row, K_W_A), per_seq), hbm, hbm,
            pl.BlockSpec((1,) + knew.shape[1:], per_seq),
            pl.BlockSpec((1,) + vnew.shape[1:], per_seq),
            lam_spec, lam_spec, lam_spec, lam_spec,
        ],
        out_specs=[pl.BlockSpec((1, tq, H_A * VD_A), q_rows),
                   pl.BlockSpec((1, n_row // 2, VD_A), per_seq)],
        scratch_shapes=[
            pltpu.VMEM((n_slab, tq, LANES), BF16),
            pltpu.VMEM((n_slab, tq, LANES), F32),
            pltpu.VMEM((n_slab, tq, 2 * VD_A), F32),
            pltpu.VMEM((2, n_pg, K_W_A, PAGE_SIZE), F32),
            pltpu.VMEM((2, n_pg, PAGE_SIZE * KV_A, VD_A), F32),
            pltpu.SemaphoreType.DMA((2,)),
            pltpu.VMEM((n_row, LANES), F32),
            pltpu.VMEM((n_row, LANES), F32),
            pltpu.VMEM((n_row, VD_A), F32),
        ],
    )
    return pl.pallas_call(
        functools.partial(_attn_a_body, tq=tq, n_flash=n_flash, n_paged=n_paged,
                          n_pg=n_pg, n_c=n_c, t_new=t_new),
        grid_spec=grid_spec,
        out_shape=[jax.ShapeDtypeStruct((batch, seq, H_A * VD_A), BF16),
                   jax.ShapeDtypeStruct((n_seq, n_row // 2, VD_A), F32)],
        compiler_params=_cparams(("arbitrary",)),
        name="attn_a",
    )(b_tab, qi_tab, kj_tab, page_table.reshape(-1), q3, k3, v3, qbd, ck, cv, knew, vnew, *lams)


def _out_a_body(o_ref, gate_ref, x_ref, sg_ref, w_ref, y_ref):
    sg = sg_ref[...]
    us = []
    for j in range(H_A):
        sl = slice(j * VD_A, (j + 1) * VD_A)
        oj = o_ref[:, sl].astype(F32)
        ms = jnp.mean(oj * oj, axis=-1, keepdims=True)
        yj = ((oj * lax.rsqrt(ms + EPS)) * sg) * (1.0 - LAMBDA_INIT_A)
        gj = gate_ref[:, sl].astype(F32)
        us.append((yj * (gj * jax.nn.sigmoid(gj))).astype(BF16))
    u = jnp.concatenate(us, axis=1)
    y_ref[...] = x_ref[...] + jnp.dot(u, w_ref[...], preferred_element_type=F32)


def _out_a(o, gate, x, subln_g, w_bf):
    n = x.shape[0]
    tm = min(TM, n)
    row = lambda i: (i, 0)
    fixed = lambda i: (0, 0)
    return pl.pallas_call(
        _out_a_body,
        grid=(n // tm,),
        in_specs=[
            pl.BlockSpec((tm, D_MODEL), row),
            pl.BlockSpec((tm, D_MODEL), row),
            pl.BlockSpec((tm, D_MODEL), row),
            pl.BlockSpec((1, VD_A), fixed),
            pl.BlockSpec((D_MODEL, D_MODEL), fixed),
        ],
        out_specs=pl.BlockSpec((tm, D_MODEL), row),
        out_shape=jax.ShapeDtypeStruct((n, D_MODEL), F32),
        compiler_params=_cparams(("parallel",)),
        name="out_a",
    )(o, gate, x, subln_g.reshape(1, -1), w_bf)


def _sink_softmax(s, sink_rows):
    sk = sink_rows * LOG2E
    m = jnp.maximum(jnp.max(s, axis=1, keepdims=True), sk)
    e = jnp.exp2(s - _lane_tile(m, s.shape[1] // LANES))
    den = jnp.sum(e, axis=1, keepdims=True) + jnp.exp2(sk - m)
    return e / _lane_tile(den, s.shape[1] // LANES)


def _stack_heads(slabs, n_t):
    lane_head = lax.broadcasted_iota(jnp.int32, (n_t, K_W_B), 1) >> (HD_B.bit_length() - 1)
    return jnp.concatenate(
        [jnp.where(lane_head == h, slabs[g], jnp.zeros_like(slabs[g]))
         for h in range(KV_B) for g in range(len(slabs))], axis=0)


def _unstack_heads(pv, n_g, n_t):
    lane_head = lax.broadcasted_iota(jnp.int32, (n_t, K_W_B), 1) >> (HD_B.bit_length() - 1)
    outs = []
    for g in range(n_g):
        acc = jnp.zeros((n_t, K_W_B), F32)
        for h in range(KV_B):
            r0 = (h * n_g + g) * n_t
            acc = jnp.where(lane_head == h, pv[r0:r0 + n_t], acc)
        outs.append(acc)
    return outs


def _attn_b_body(q_ref, kc_ref, kp_ref, vc_ref, vp_ref, sink_ref, o_ref, *, tq):
    first_tile = pl.program_id(1) == 0
    w = WINDOW
    head_rows = G_B * w
    sink_rows = sink_ref[...]
    t = lax.broadcasted_iota(jnp.int32, (w, 2 * w), 0)
    col = lax.broadcasted_iota(jnp.int32, (w, 2 * w), 1)
    in_window = (col > t) & (col <= t + w)
    bias = jnp.where(in_window, 0.0, -jnp.inf)
    bias_no_prev = jnp.where(in_window & (col >= w), 0.0, -jnp.inf)
    lane_head = lax.broadcasted_iota(jnp.int32, (2 * w, V_W_B), 1) >> (HD_B.bit_length() - 1)
    for j in range(tq // w):
        if j == 0:
            k2 = jnp.concatenate([kp_ref[0], kc_ref[0, :w]], axis=0)
            v2 = jnp.concatenate([vp_ref[0], vc_ref[0, :w]], axis=0)
            b = jnp.where(first_tile, bias_no_prev, bias)
        else:
            k2 = kc_ref[0, (j - 1) * w:(j + 1) * w]
            v2 = vc_ref[0, (j - 1) * w:(j + 1) * w]
            b = bias
        qsub = q_ref[0, j * w:(j + 1) * w, :]
        qst = _stack_heads([qsub[:, g * K_W_B:(g + 1) * K_W_B] for g in range(G_B)], w)
        s = lax.dot_general(qst, k2, NT_DIMS, preferred_element_type=F32)
        p = _sink_softmax(s + jnp.tile(b, (H_B, 1)), sink_rows).astype(BF16)
        p_wide = jnp.concatenate(
            [p[h * head_rows:(h + 1) * head_rows] for h in range(KV_B)], axis=1)
        v_heads = jnp.concatenate(
            [jnp.where(lane_head == h, v2, jnp.zeros_like(v2)) for h in range(KV_B)], axis=0)
        o = jnp.dot(p_wide, v_heads, preferred_element_type=F32)
        for g in range(G_B):
            o_ref[0, j * w:(j + 1) * w, g * K_W_B:(g + 1) * K_W_B] = (
                o[g * w:(g + 1) * w].astype(o_ref.dtype))


def _attn_b_prompt(q, kb, vb, sink_rows, batch, seq):
    tq = TQ_B
    r = tq // WINDOW
    q3 = q.reshape(batch, seq, Q_W_B)
    k3 = kb.reshape(batch, seq, K_W_B)
    v3 = vb.reshape(batch, seq, V_W_B)
    cur = lambda b, i: (b, i, 0)
    prev = lambda b, i: (b, jnp.maximum(i * r - 1, 0), 0)
    return pl.pallas_call(
        functools.partial(_attn_b_body, tq=tq),
        grid=(batch, seq // tq),
        in_specs=[
            pl.BlockSpec((1, tq, Q_W_B), cur),
            pl.BlockSpec((1, tq, K_W_B), cur),
            pl.BlockSpec((1, WINDOW, K_W_B), prev),
            pl.BlockSpec((1, tq, V_W_B), cur),
            pl.BlockSpec((1, WINDOW, V_W_B), prev),
            pl.BlockSpec(sink_rows.shape, lambda b, i: (0, 0)),
        ],
        out_specs=pl.BlockSpec((1, tq, Q_W_B), cur),
        out_shape=jax.ShapeDtypeStruct((batch, seq, Q_W_B), BF16),
        compiler_params=_cparams(("parallel", "arbitrary")),
        name="attn_b_prompt",
    )(q3, k3, k3, v3, v3, sink_rows)


def _attn_b_sample_body(qbd_ref, kt_ref, vt_ref, knew_ref, vnew_ref, sink_ref,
                        o_ref, kto_ref, vto_ref, *, t_new):
    w = WINDOW
    n_row = qbd_ref.shape[1]
    lane = lax.broadcasted_iota(jnp.int32, (K_W_B, w), 1)
    t = lax.broadcasted_iota(jnp.int32, (n_row, 2 * w), 0) & (t_new - 1)
    col = lax.broadcasted_iota(jnp.int32, (n_row, 2 * w), 1)
    valid = ((col > t) & (col < w)) | ((col >= 2 * w - t_new) & (col <= t + 2 * w - t_new))
    sink_rows = sink_ref[...]

    def new_tile(new_rows):
        pad = jnp.zeros((w - new_rows.shape[0], new_rows.shape[1]), F32)
        return jnp.concatenate([pad, new_rows], axis=0).T

    def shifted(buf_t, new_t):
        return jnp.where(lane < w - t_new, pltpu.roll(buf_t, w - t_new, axis=1), new_t)

    for b in range(qbd_ref.shape[0]):
        kt, vt = kt_ref[b], vt_ref[b]
        knew_t, vnew_t = new_tile(knew_ref[b]), new_tile(vnew_ref[b])
        kto_ref[b] = shifted(kt, knew_t)
        vto_ref[b] = shifted(vt, vnew_t)
        k2t = jnp.concatenate([kt, knew_t], axis=1).astype(BF16)
        v2 = jnp.concatenate([vt, vnew_t], axis=1).T.astype(BF16)
        s = jnp.dot(qbd_ref[b], k2t, preferred_element_type=F32)
        p = _sink_softmax(jnp.where(valid, s, -jnp.inf), sink_rows)
        pv = jnp.dot(p.astype(BF16), v2, preferred_element_type=F32)
        o_ref[b] = _unstack_heads(pv, 1, n_row // KV_B)[0]


def _attn_b_sample(qbd, kbuf_t, vbuf_t, knew, vnew, sink_rows, t_new):
    n_seq, n_row, _ = qbd.shape
    nb = math.gcd(n_seq, SEQS_PER_STEP_B)
    per_seq = lambda s: (s, 0, 0)
    buf_spec = pl.BlockSpec((nb, K_W_B, WINDOW), per_seq)
    buf_shape = jax.ShapeDtypeStruct((n_seq, K_W_B, WINDOW), F32)
    return pl.pallas_call(
        functools.partial(_attn_b_sample_body, t_new=t_new),
        grid=(n_seq // nb,),
        in_specs=[
            pl.BlockSpec((nb, n_row, K_W_B), per_seq),
            buf_spec,
            buf_spec,
            pl.BlockSpec((nb,) + knew.shape[1:], per_seq),
            pl.BlockSpec((nb,) + vnew.shape[1:], per_seq),
            pl.BlockSpec(sink_rows.shape, lambda s: (0, 0)),
        ],
        out_specs=[pl.BlockSpec((nb, n_row // KV_B, K_W_B), per_seq), buf_spec, buf_spec],
        out_shape=[jax.ShapeDtypeStruct((n_seq, n_row // KV_B, K_W_B), F32), buf_shape, buf_shape],
        compiler_params=_cparams(("parallel",)),
        name="attn_b_sample",
    )(qbd, kbuf_t, vbuf_t, knew, vnew, sink_rows)


def _out_b_body(o_ref, gate_ref, x_ref, w_ref, nf_ref, y_ref):
    g = gate_ref[...].astype(F32)
    u = (o_ref[...].astype(F32) * (g * jax.nn.sigmoid(g))).astype(BF16)
    y = x_ref[...] + jnp.dot(u, w_ref[...], preferred_element_type=F32)
    ms = jnp.mean(y * y, axis=-1, keepdims=True)
    y_ref[...] = (y * lax.rsqrt(ms + EPS)) * nf_ref[...]


def _out_b(o, gate, x, w_bf, norm_f):
    n = x.shape[0]
    tm = min(TM, n)
    row = lambda i: (i, 0)
    fixed = lambda i: (0, 0)
    return pl.pallas_call(
        _out_b_body,
        grid=(n // tm,),
        in_specs=[
            pl.BlockSpec((tm, D_MODEL), row),
            pl.BlockSpec((tm, D_MODEL), row),
            pl.BlockSpec((tm, D_MODEL), row),
            pl.BlockSpec((D_MODEL, D_MODEL), fixed),
            pl.BlockSpec((1, D_MODEL), fixed),
        ],
        out_specs=pl.BlockSpec((tm, D_MODEL), row),
        out_shape=jax.ShapeDtypeStruct((n, D_MODEL), F32),
        compiler_params=_cparams(("parallel",)),
        name="out_b",
    )(o, gate, x, w_bf, norm_f.reshape(1, -1))


def _pad_rows(x, rows):
    return jnp.pad(x, ((0, 0), (0, rows - x.shape[1]), (0, 0)))


def kernel(x_prompt, x_sample, cache_a_k, cache_a_v, page_table, state_b_k, state_b_v,
           norm_a, w_in_a, lambda_q1, lambda_k1, lambda_q2, lambda_k2, subln_a, w_out_a,
           norm_b, w_in_b, b_in_b, sinks_b, w_out_b, norm_f):
    batch, seq, _ = x_prompt.shape
    n_seq, t_new, _ = x_sample.shape
    past = page_table.shape[1] * PAGE_SIZE
    xp = x_prompt.reshape(batch * seq, D_MODEL)
    xs = x_sample.reshape(n_seq * t_new, D_MODEL)

    cos_p, sin_p = _rope_tables(jnp.arange(seq))
    cos_s, sin_s = _rope_tables(past + jnp.arange(t_new))
    cos_s = jnp.tile(cos_s, (n_seq, 1))
    sin_s = jnp.tile(sin_s, (n_seq, 1))
    lams = [v.reshape(1, HD_A) for v in (lambda_q1, lambda_k1, lambda_q2, lambda_k2)]

    wa = w_in_a.astype(BF16)
    ba = jnp.zeros((w_in_a.shape[1],), F32)
    woa = w_out_a.astype(BF16)
    dims_a = dict(q_w=Q_W_A, k_w=K_W_A, v_w=V_W_A)
    q_p, kt_p, vh_p, kb_p, vb_p, gate_p = _proj(
        xp, norm_a, wa, ba, cos_p, sin_p, seq, k_layout="transposed", v_layout="head_rows",
        vb_layout="rows", **dims_a)
    q_s, k_s, vh_s, _, _, gate_s = _proj(
        xs, norm_a, wa, ba, cos_s, sin_s, n_seq * t_new, k_layout="rows", v_layout="head_rows",
        vb_layout="rows", **dims_a)

    qr = q_s.reshape(n_seq, t_new, KV_A, G_A, 2, HD_A).transpose(0, 2, 4, 3, 1, 5)
    qr = qr.reshape(n_seq, 2 * KV_A * G_A * t_new, HD_A)
    blk = jnp.repeat(jnp.arange(2 * KV_A), G_A * t_new)
    onehot = (blk[:, None] == jnp.arange(2 * KV_A)[None, :]).astype(BF16)
    qbd_a = (qr[:, :, None, :] * onehot[None, :, :, None]).reshape(n_seq, -1, K_W_A)
    knew_a = _pad_rows(k_s.reshape(n_seq, t_new, K_W_A), 8)
    vnew_a = _pad_rows(vh_s.reshape(n_seq, t_new, V_W_A), 8)
    o_p, o_s = _attn_a(q_p, kb_p, vb_p, batch, seq,
                       page_table, qbd_a, cache_a_k, cache_a_v, knew_a, vnew_a, lams)
    xp1 = _out_a(o_p.reshape(batch * seq, D_MODEL), gate_p, xp, subln_a, woa)
    o_s = o_s.reshape(n_seq, KV_A, G_A, t_new, VD_A).transpose(0, 3, 1, 2, 4)
    xs1 = _out_a(o_s.reshape(n_seq * t_new, D_MODEL), gate_s, xs, subln_a, woa)

    perm = (jnp.arange(G_B)[:, None, None] * HD_B
            + jnp.arange(KV_B)[None, :, None] * (G_B * HD_B)
            + jnp.arange(HD_B)[None, None, :]).reshape(-1)
    gate0 = Q_W_B + K_W_B + V_W_B
    cols = jnp.concatenate([perm, jnp.arange(Q_W_B, gate0), gate0 + perm])
    wb = w_in_b[:, cols].astype(BF16)
    bb = b_in_b[cols]
    wob = w_out_b[perm, :].astype(BF16)
    dims_b = dict(q_w=Q_W_B, k_w=K_W_B, v_w=V_W_B)
    qb_p, kt_bp, vt_bp, kbb_p, vbb_p, gb_p = _proj(
        xp1, norm_b, wb, bb, cos_p, sin_p, seq, k_layout="transposed", v_layout="transposed",
        vb_layout="rows", **dims_b)
    qb_s, kf_s, vf_s, _, _, gb_s = _proj(
        xs1, norm_b, wb, bb, cos_s, sin_s, n_seq * t_new, k_layout="rows", v_layout="rows",
        vb_layout="rows", **dims_b)

    sink_p = jnp.broadcast_to(jnp.repeat(sinks_b, WINDOW)[:, None], (H_B * WINDOW, LANES))
    ob_p = _attn_b_prompt(qb_p, kbb_p, vbb_p, sink_p, batch, seq)
    y_p = _out_b(ob_p.reshape(batch * seq, D_MODEL), gb_p, xp1, wob, norm_f)

    qr = qb_s.reshape(n_seq, t_new, G_B, KV_B, HD_B).transpose(0, 3, 2, 1, 4)
    qr = qr.reshape(n_seq, KV_B * G_B * t_new, HD_B)
    blk = jnp.repeat(jnp.arange(KV_B), G_B * t_new)
    onehot = (blk[:, None] == jnp.arange(KV_B)[None, :]).astype(BF16)
    qbd_b = (qr[:, :, None, :] * onehot[None, :, :, None]).reshape(n_seq, -1, K_W_B)
    sink_s = jnp.broadcast_to(jnp.repeat(sinks_b, t_new)[:, None], (H_B * t_new, LANES))
    front = ((0, 0), (8 - t_new, 0), (0, 0))
    knew_b = jnp.pad(kf_s.reshape(n_seq, t_new, K_W_B), front)
    vnew_b = jnp.pad(vf_s.reshape(n_seq, t_new, V_W_B), front)
    to_minor = lambda buf: buf.transpose(0, 2, 3, 1).reshape(buf.shape[0], -1, WINDOW)
    from_minor = lambda t: t.reshape(t.shape[0], KV_B, HD_B, WINDOW).transpose(0, 3, 1, 2)
    ob_s, kt_bs, vt_bs = _attn_b_sample(qbd_b, to_minor(state_b_k), to_minor(state_b_v),
                                        knew_b, vnew_b, sink_s, t_new)
    ob_s = ob_s.reshape(n_seq, G_B, t_new, K_W_B).transpose(0, 2, 1, 3)
    y_s = _out_b(ob_s.reshape(n_seq * t_new, D_MODEL), gb_s, xs1, wob, norm_f)

    return (
        y_p.reshape(batch, seq, D_MODEL),
        y_s.reshape(n_seq, t_new, D_MODEL),
        kt_p.reshape(batch, KV_A, 2, HD_A, seq).transpose(0, 4, 1, 2, 3),
        vh_p.reshape(batch, seq, KV_A, VD_A),
        k_s.reshape(n_seq, t_new, KV_A, 2, HD_A),
        vh_s.reshape(n_seq, t_new, KV_A, VD_A),
        from_minor(kt_bp[:, :, seq - WINDOW:]),
        from_minor(vt_bp[:, :, seq - WINDOW:]),
        from_minor(kt_bs),
        from_minor(vt_bs),
    )
```

```python
import functools
import math

import jax
import jax.numpy as jnp
from jax import lax
from jax.experimental import pallas as pl
from jax.experimental.pallas import tpu as pltpu

F32 = jnp.float32
BF16 = jnp.bfloat16

D_MODEL = 1024
PAGE_SIZE = 128
EPS = 1e-5
ROPE_THETA = 500000.0
H_A, HD_A, KV_A = 8, 64, 2
VD_A = 2 * HD_A
G_A = H_A // KV_A
Q_W_A = H_A * 2 * HD_A
K_W_A = KV_A * 2 * HD_A
V_W_A = KV_A * VD_A
LAMBDA_INIT_A = 0.8 - 0.6 * math.exp(-0.3 * 0)
H_B, HD_B, KV_B = 16, 64, 4
G_B = H_B // KV_B
WINDOW = 128
Q_W_B = H_B * HD_B
K_W_B = KV_B * HD_B
V_W_B = KV_B * HD_B
ROT = HD_A // 4
HALF = ROT // 2

LOG2E = math.log2(math.e)
Q_SCALE = (HD_A ** -0.5) * LOG2E

LANES = 128
VMEM_LIMIT = 48 * 1024 * 1024

TM = 512
TM_OUT = 1024
TQ_A = 512
TQ_B = 512
PAGES_PER_STEP = 32
SEQS_PER_STEP_B = 8

NT_DIMS = (((1,), (1,)), ((), ()))


def _lane_tile(x, n):
    return x if n == 1 else jnp.concatenate([x] * n, axis=1)


def _cparams(sem):
    return pltpu.CompilerParams(dimension_semantics=sem, vmem_limit_bytes=VMEM_LIMIT)


def _rope_tables(pos):
    inv = jnp.power(ROPE_THETA, -jnp.arange(HALF, dtype=F32) * 2.0 / ROT)
    ang = pos.astype(F32)[:, None] * inv[None, :]
    cos, sin = jnp.cos(ang), jnp.sin(ang)
    ones = jnp.ones((pos.shape[0], HD_A - ROT), F32)
    cos_h = jnp.concatenate([cos, cos, ones], axis=1)
    sin_h = jnp.concatenate([-sin, sin, 0.0 * ones], axis=1)
    return jnp.tile(cos_h, (1, LANES // HD_A)), jnp.tile(sin_h, (1, LANES // HD_A))


def _proj_body(x_ref, g_ref, w_ref, b_ref, cos_ref, sin_ref,
               q_ref, k_ref, v_ref, kb_ref, vb_ref, gate_ref, *, q_w, k_w, v_w,
               k_layout, v_layout, vb_layout):
    x = x_ref[...]
    ms = jnp.mean(x * x, axis=-1, keepdims=True)
    xn = (x * lax.rsqrt(ms + EPS)) * g_ref[...]
    z = jnp.dot(xn.astype(BF16), w_ref[...], preferred_element_type=F32) + b_ref[...]
    cos = cos_ref[...]
    sin = sin_ref[...]
    lane = lax.broadcasted_iota(jnp.int32, cos.shape, 1)
    first_half = (lane & (HD_A - 1)) < HALF

    def rope(blk):
        partner = jnp.where(first_half, pltpu.roll(blk, LANES - HALF, axis=1),
                            pltpu.roll(blk, HALF, axis=1))
        return blk * cos + partner * sin

    for j in range(q_w // LANES):
        sl = slice(j * LANES, (j + 1) * LANES)
        q_ref[:, sl] = (rope(z[:, sl]) * Q_SCALE).astype(BF16)
    kr = jnp.concatenate(
        [rope(z[:, q_w + j * LANES:q_w + (j + 1) * LANES]) for j in range(k_w // LANES)], axis=1)
    v = z[:, q_w + k_w:q_w + k_w + v_w]
    kb_ref[...] = kr.astype(BF16)
    gate_ref[...] = z[:, q_w + k_w + v_w:].astype(gate_ref.dtype)
    tm = kr.shape[0]
    if k_layout == "transposed":
        k_ref[0] = kr.T
    else:
        k_ref[...] = kr
    if "transposed" in (v_layout, vb_layout):
        vt = v.T
    if vb_layout == "transposed":
        vb_ref[0] = vt.astype(BF16)
    else:
        vb_ref[...] = v.astype(BF16)
    if v_layout == "transposed":
        v_ref[0] = vt
    elif v_layout == "head_rows":
        n_h = v_w // LANES
        for h in range(n_h):
            v_ref[pl.ds(h, tm, stride=n_h), :] = v[:, h * LANES:(h + 1) * LANES]
    else:
        v_ref[...] = v


def _proj(x, norm_g, w_bf, bias, cos_t, sin_t, seq, *, q_w, k_w, v_w,
          k_layout, v_layout, vb_layout):
    n = x.shape[0]
    tm = min(TM, n, seq)
    width = w_bf.shape[1]
    gate_w = width - q_w - k_w - v_w
    n_tab = cos_t.shape[0] // tm
    per_seq = seq // tm
    row = lambda i: (i, 0)
    fixed = lambda i: (0, 0)
    tab = lambda i: (i % n_tab, 0)

    def out(w, layout, dtype):
        if layout == "transposed":
            return (pl.BlockSpec((1, w, tm), lambda i: (i // per_seq, 0, i % per_seq)),
                    jax.ShapeDtypeStruct((n // seq, w, seq), dtype))
        if layout == "head_rows":
            n_h = w // LANES
            return (pl.BlockSpec((tm * n_h, LANES), row),
                    jax.ShapeDtypeStruct((n * n_h, LANES), dtype))
        return pl.BlockSpec((tm, w), row), jax.ShapeDtypeStruct((n, w), dtype)

    k_spec, k_shape = out(k_w, k_layout, F32)
    v_spec, v_shape = out(v_w, v_layout, F32)
    vb_spec, vb_shape = out(v_w, vb_layout, BF16)
    return pl.pallas_call(
        functools.partial(_proj_body, q_w=q_w, k_w=k_w, v_w=v_w,
                          k_layout=k_layout, v_layout=v_layout, vb_layout=vb_layout),
        grid=(n // tm,),
        in_specs=[
            pl.BlockSpec((tm, D_MODEL), row),
            pl.BlockSpec((1, D_MODEL), fixed),
            pl.BlockSpec((D_MODEL, width), fixed),
            pl.BlockSpec((1, width), fixed),
            pl.BlockSpec((tm, LANES), tab),
            pl.BlockSpec((tm, LANES), tab),
        ],
        out_specs=[
            pl.BlockSpec((tm, q_w), row),
            k_spec,
            v_spec,
            pl.BlockSpec((tm, k_w), row),
            vb_spec,
            pl.BlockSpec((tm, gate_w), row),
        ],
        out_shape=[
            jax.ShapeDtypeStruct((n, q_w), BF16),
            k_shape,
            v_shape,
            jax.ShapeDtypeStruct((n, k_w), BF16),
            vb_shape,
            jax.ShapeDtypeStruct((n, gate_w), BF16),
        ],
        compiler_params=_cparams(("parallel",)),
        name="proj",
    )(x, norm_g.reshape(1, -1), w_bf, bias.reshape(1, -1), cos_t, sin_t)


def _lambda_full(lq1_ref, lk1_ref, lq2_ref, lk2_ref):
    s1 = jnp.sum(lq1_ref[...] * lk1_ref[...], axis=1, keepdims=True)
    s2 = jnp.sum(lq2_ref[...] * lk2_ref[...], axis=1, keepdims=True)
    return jnp.exp(s1) - jnp.exp(s2) + LAMBDA_INIT_A


def _attn_a_body(b_ref, qi_ref, kj_ref, pt_ref,
                 q_ref, k_ref, v_ref, qbd_ref, k_hbm, v_hbm, knew_ref, vnew_ref,
                 lq1_ref, lk1_ref, lq2_ref, lk2_ref,
                 o_ref, os_ref,
                 qs_ref, m_ref, acc_ref, kbuf, vbuf, sem, ms_ref, ls_ref, accs_ref,
                 *, tq, n_flash, n_paged, n_pg, n_c, t_new):
    step = pl.program_id(0)
    qi = qi_ref[step]
    kj = kj_ref[step]
    flash_on = step < n_flash
    paged_on = step < n_paged
    c = lax.rem(step, n_c)
    slot = step & 1
    per_head = 2 * G_A
    qbd = qbd_ref[0]
    n_row = qbd.shape[0]
    h_rows = n_row // KV_A

    def page_copies(page_of, sl):
        cps = []
        for i in range(n_pg):
            page = page_of(i)
            cps.append(pltpu.make_async_copy(k_hbm.at[page], kbuf.at[sl, i], sem.at[sl]))
            cps.append(pltpu.make_async_copy(v_hbm.at[page], vbuf.at[sl, i], sem.at[sl]))
        return cps

    @pl.when(step == 0)
    def _first_fetch():
        for cp in page_copies(lambda i: pt_ref[i], 0):
            cp.start()

    @pl.when(step + 1 < n_paged)
    def _prefetch_next():
        for cp in page_copies(lambda i: pt_ref[(step + 1) * n_pg + i], 1 - slot):
            cp.start()

    @pl.when(paged_on)
    def _wait_pages():
        for cp in page_copies(lambda i: 0, slot):
            cp.wait()

    @pl.when(flash_on & (kj == 0))
    def _init_flash():
        lane = lax.broadcasted_iota(jnp.int32, (tq, LANES), 1)
        for h in range(KV_A):
            for g in range(G_A):
                slab = q_ref[0, :, (h * G_A + g) * LANES:(h * G_A + g + 1) * LANES]
                zero = jnp.zeros_like(slab)
                qs_ref[h * per_head + g] = jnp.where(lane < HD_A, slab, zero)
                qs_ref[h * per_head + G_A + g] = jnp.where(lane >= HD_A, slab, zero)
        m_ref[...] = jnp.full(m_ref.shape, -jnp.inf, F32)
        acc_ref[...] = jnp.zeros(acc_ref.shape, F32)

    @pl.when(paged_on & (c == 0))
    def _init_paged():
        ms_ref[...] = jnp.full(ms_ref.shape, -jnp.inf, F32)
        ls_ref[...] = jnp.zeros(ls_ref.shape, F32)
        accs_ref[...] = jnp.zeros(accs_ref.shape, F32)

    def flash_update(masked):
        ones = jnp.ones((tq, LANES), BF16)
        for h in range(KV_A):
            k = k_ref[0, :, h * LANES:(h + 1) * LANES]
            v_ones = jnp.concatenate([v_ref[0, :, h * VD_A:(h + 1) * VD_A], ones], axis=1)
            for i in range(h * per_head, (h + 1) * per_head):
                s = lax.dot_general(qs_ref[i], k, NT_DIMS, preferred_element_type=F32)
                if masked:
                    row = lax.broadcasted_iota(jnp.int32, s.shape, 0)
                    col = lax.broadcasted_iota(jnp.int32, s.shape, 1)
                    s = jnp.where(col <= row, s, -jnp.inf)
                m_prev = m_ref[i]
                m_new = jnp.maximum(m_prev, jnp.max(s, axis=1, keepdims=True))
                alpha = jnp.exp2(m_prev - m_new)
                p = jnp.exp2(s - _lane_tile(m_new, tq // LANES)).astype(BF16)
                acc_ref[i] = (_lane_tile(alpha, 2) * acc_ref[i]
                              + jnp.dot(p, v_ones, preferred_element_type=F32))
                m_ref[i] = m_new

    def flash_finish():
        lam = _lambda_full(lq1_ref, lk1_ref, lq2_ref, lk2_ref)
        for h in range(KV_A):
            for g in range(G_A):
                a1 = acc_ref[h * per_head + g]
                a2 = acc_ref[h * per_head + G_A + g]
                o1 = a1[:, :VD_A] / a1[:, VD_A:]
                o2 = a2[:, :VD_A] / a2[:, VD_A:]
                o_ref[0, :, (h * G_A + g) * LANES:(h * G_A + g + 1) * LANES] = (
                    o1 - lam * o2).astype(o_ref.dtype)

    def online_update(s, v_of, width):
        m_prev = ms_ref[...]
        m_new = jnp.maximum(m_prev, jnp.max(s, axis=1, keepdims=True))
        alpha = jnp.exp2(m_prev - m_new)
        p = jnp.exp2(s - _lane_tile(m_new, s.shape[1] // LANES))
        ls_ref[...] = alpha * ls_ref[...] + jnp.sum(p, axis=1, keepdims=True)
        p = p.astype(BF16)
        pvs = []
        for h in range(KV_A):
            pv = jnp.zeros((h_rows, VD_A), F32)
            for i in range(s.shape[1] // width):
                pv = pv + jnp.dot(p[h * h_rows:(h + 1) * h_rows, i * width:(i + 1) * width],
                                  v_of(i, h), preferred_element_type=F32)
            pvs.append(pv)
        accs_ref[...] = alpha * accs_ref[...] + jnp.concatenate(pvs, axis=0)
        ms_ref[...] = m_new

    def paged_update():
        def k_pair(j):
            return jnp.concatenate([kbuf[slot, 2 * j], kbuf[slot, 2 * j + 1]], axis=1).astype(BF16)

        def v_pair(j, h):
            return jnp.concatenate(
                [vbuf[slot, 2 * j + r, pl.ds(h, PAGE_SIZE, stride=KV_A), :] for r in range(2)],
                axis=0).astype(BF16)

        s = jnp.concatenate(
            [jnp.dot(qbd, k_pair(j), preferred_element_type=F32) for j in range(n_pg // 2)], axis=1)
        online_update(jnp.where(paged_on, s, -jnp.inf), v_pair, 2 * PAGE_SIZE)

    @pl.when(flash_on & (kj < qi))
    def _off_diagonal():
        flash_update(False)
        paged_update()

    @pl.when(flash_on & (kj == qi))
    def _diagonal():
        flash_update(True)
        flash_finish()
        paged_update()

    @pl.when(jnp.logical_not(flash_on))
    def _sample_only():
        paged_update()

    @pl.when(paged_on & (c == n_c - 1))
    def _paged_finish():
        pad = jnp.zeros((PAGE_SIZE - knew_ref.shape[1], knew_ref.shape[2]), F32)
        kn = jnp.concatenate([knew_ref[0], pad], axis=0).astype(BF16)
        vn = jnp.concatenate([vnew_ref[0], pad], axis=0).astype(BF16)
        sn = lax.dot_general(qbd, kn, NT_DIMS, preferred_element_type=F32)
        t = lax.broadcasted_iota(jnp.int32, sn.shape, 0) & (t_new - 1)
        col = lax.broadcasted_iota(jnp.int32, sn.shape, 1)
        online_update(jnp.where(col <= t, sn, -jnp.inf),
                      lambda i, h: vn[:, h * VD_A:(h + 1) * VD_A], PAGE_SIZE)
        lam = _lambda_full(lq1_ref, lk1_ref, lq2_ref, lk2_ref)
        o = accs_ref[...] / ls_ref[...]
        half = h_rows // 2
        os_ref[0] = jnp.concatenate(
            [o[h * h_rows:h * h_rows + half] - lam * o[h * h_rows + half:(h + 1) * h_rows]
             for h in range(KV_A)], axis=0)


def _attn_a(q, kb, vb, batch, seq, page_table, qbd, cache_k, cache_v, knew, vnew, lams):
    tq = TQ_A
    nq = seq // tq
    n_seq, n_pages = page_table.shape
    n_pool = cache_k.shape[0]
    n_pg = PAGES_PER_STEP
    n_c = n_pages // n_pg
    n_row = qbd.shape[1]
    t_new = n_row // (2 * KV_A * G_A)
    pairs = [(b, i, j) for b in range(batch) for i in range(nq) for j in range(i + 1)]
    n_flash, n_paged = len(pairs), n_seq * n_c
    n_steps = max(n_flash, n_paged)
    pairs = pairs + [pairs[-1]] * (n_steps - n_flash)
    b_tab, qi_tab, kj_tab = (jnp.asarray([p[a] for p in pairs], jnp.int32) for a in range(3))
    q3 = q.reshape(batch, seq, Q_W_A)
    k3 = kb.reshape(batch, seq, K_W_A)
    v3 = vb.reshape(batch, seq, V_W_A)
    ck = jnp.transpose(cache_k, (0, 2, 3, 4, 1)).reshape(n_pool, K_W_A, PAGE_SIZE)
    cv = cache_v.reshape(n_pool, PAGE_SIZE * KV_A, VD_A)
    n_slab = KV_A * 2 * G_A
    q_rows = lambda s, b, qi, kj, pt: (b[s], qi[s], 0)
    k_rows = lambda s, b, qi, kj, pt: (b[s], kj[s], 0)
    per_seq = lambda s, b, qi, kj, pt: (jnp.minimum(s // n_c, n_seq - 1), 0, 0)
    lam_spec = pl.BlockSpec((1, HD_A), lambda s, b, qi, kj, pt: (0, 0))
    hbm = pl.BlockSpec(memory_space=pl.ANY)
    grid_spec = pltpu.PrefetchScalarGridSpec(
        num_scalar_prefetch=4,
        grid=(n_steps,),
        in_specs=[
            pl.BlockSpec((1, tq, Q_W_A), q_rows),
            pl.BlockSpec((1, tq, K_W_A), k_rows),
            pl.BlockSpec((1, tq, V_W_A), k_rows),
            pl.BlockSpec((1, n_row, K_W_A), per_seq), hbm, hbm,
            pl.BlockSpec((1,) + knew.shape[1:], per_seq),
            pl.BlockSpec((1,) + vnew.shape[1:], per_seq),
            lam_spec, lam_spec, lam_spec, lam_spec,
        ],
        out_specs=[pl.BlockSpec((1, tq, H_A * VD_A), q_rows),
                   pl.BlockSpec((1, n_row // 2, VD_A), per_seq)],
        scratch_shapes=[
            pltpu.VMEM((n_slab, tq, LANES), BF16),
            pltpu.VMEM((n_slab, tq, LANES), F32),
            pltpu.VMEM((n_slab, tq, 2 * VD_A), F32),
            pltpu.VMEM((2, n_pg, K_W_A, PAGE_SIZE), F32),
            pltpu.VMEM((2, n_pg, PAGE_SIZE * KV_A, VD_A), F32),
            pltpu.SemaphoreType.DMA((2,)),
            pltpu.VMEM((n_row, LANES), F32),
            pltpu.VMEM((n_row, LANES), F32),
            pltpu.VMEM((n_row, VD_A), F32),
        ],
    )
    return pl.pallas_call(
        functools.partial(_attn_a_body, tq=tq, n_flash=n_flash, n_paged=n_paged,
                          n_pg=n_pg, n_c=n_c, t_new=t_new),
        grid_spec=grid_spec,
        out_shape=[jax.ShapeDtypeStruct((batch, seq, H_A * VD_A), BF16),
                   jax.ShapeDtypeStruct((n_seq, n_row // 2, VD_A), F32)],
        compiler_params=_cparams(("arbitrary",)),
        name="attn_a",
    )(b_tab, qi_tab, kj_tab, page_table.reshape(-1), q3, k3, v3, qbd, ck, cv, knew, vnew, *lams)


def _out_a_body(o_ref, gate_ref, x_ref, sg_ref, w_ref, y_ref):
    sg = sg_ref[...]
    us = []
    for j in range(H_A):
        sl = slice(j * VD_A, (j + 1) * VD_A)
        oj = o_ref[:, sl].astype(F32)
        ms = jnp.mean(oj * oj, axis=-1, keepdims=True)
        yj = ((oj * lax.rsqrt(ms + EPS)) * sg) * (1.0 - LAMBDA_INIT_A)
        gj = gate_ref[:, sl].astype(F32)
        us.append((yj * (gj * jax.nn.sigmoid(gj))).astype(BF16))
    u = jnp.concatenate(us, axis=1)
    y_ref[...] = x_ref[...] + jnp.dot(u, w_ref[...], preferred_element_type=F32)


def _out_a(o, gate, x, subln_g, w_bf):
    n = x.shape[0]
    tm = min(TM_OUT, n)
    row = lambda i: (i, 0)
    fixed = lambda i: (0, 0)
    return pl.pallas_call(
        _out_a_body,
        grid=(n // tm,),
        in_specs=[
            pl.BlockSpec((tm, D_MODEL), row),
            pl.BlockSpec((tm, D_MODEL), row),
            pl.BlockSpec((tm, D_MODEL), row),
            pl.BlockSpec((1, VD_A), fixed),
            pl.BlockSpec((D_MODEL, D_MODEL), fixed),
        ],
        out_specs=pl.BlockSpec((tm, D_MODEL), row),
        out_shape=jax.ShapeDtypeStruct((n, D_MODEL), F32),
        compiler_params=_cparams(("parallel",)),
        name="out_a",
    )(o, gate, x, subln_g.reshape(1, -1), w_bf)


def _sink_softmax(s, sink_rows):
    sk = sink_rows * LOG2E
    m = jnp.maximum(jnp.max(s, axis=1, keepdims=True), sk)
    e = jnp.exp2(s - _lane_tile(m, s.shape[1] // LANES))
    den = jnp.sum(e, axis=1, keepdims=True) + jnp.exp2(sk - m)
    return e / _lane_tile(den, s.shape[1] // LANES)


def _stack_heads(slabs, n_t):
    lane_head = lax.broadcasted_iota(jnp.int32, (n_t, K_W_B), 1) >> (HD_B.bit_length() - 1)
    return jnp.concatenate(
        [jnp.where(lane_head == h, slabs[g], jnp.zeros_like(slabs[g]))
         for h in range(KV_B) for g in range(len(slabs))], axis=0)


def _unstack_heads(pv, n_g, n_t):
    lane_head = lax.broadcasted_iota(jnp.int32, (n_t, K_W_B), 1) >> (HD_B.bit_length() - 1)
    outs = []
    for g in range(n_g):
        acc = jnp.zeros((n_t, K_W_B), F32)
        for h in range(KV_B):
            r0 = (h * n_g + g) * n_t
            acc = jnp.where(lane_head == h, pv[r0:r0 + n_t], acc)
        outs.append(acc)
    return outs


def _attn_b_body(q_ref, kc_ref, kp_ref, vc_ref, vp_ref, sink_ref, o_ref, *, tq):
    first_tile = pl.program_id(1) == 0
    w = WINDOW
    head_rows = G_B * w
    sink_rows = sink_ref[...]
    t = lax.broadcasted_iota(jnp.int32, (w, 2 * w), 0)
    col = lax.broadcasted_iota(jnp.int32, (w, 2 * w), 1)
    in_window = (col > t) & (col <= t + w)
    bias = jnp.where(in_window, 0.0, -jnp.inf)
    bias_no_prev = jnp.where(in_window & (col >= w), 0.0, -jnp.inf)
    lane_head = lax.broadcasted_iota(jnp.int32, (2 * w, V_W_B), 1) >> (HD_B.bit_length() - 1)
    for j in range(tq // w):
        if j == 0:
            k2 = jnp.concatenate([kp_ref[0], kc_ref[0, :w]], axis=0)
            v2 = jnp.concatenate([vp_ref[0], vc_ref[0, :w]], axis=0)
            b = jnp.where(first_tile, bias_no_prev, bias)
        else:
            k2 = kc_ref[0, (j - 1) * w:(j + 1) * w]
            v2 = vc_ref[0, (j - 1) * w:(j + 1) * w]
            b = bias
        qsub = q_ref[0, j * w:(j + 1) * w, :]
        qst = _stack_heads([qsub[:, g * K_W_B:(g + 1) * K_W_B] for g in range(G_B)], w)
        s = lax.dot_general(qst, k2, NT_DIMS, preferred_element_type=F32)
        p = _sink_softmax(s + jnp.tile(b, (H_B, 1)), sink_rows).astype(BF16)
        p_wide = jnp.concatenate(
            [p[h * head_rows:(h + 1) * head_rows] for h in range(KV_B)], axis=1)
        v_heads = jnp.concatenate(
            [jnp.where(lane_head == h, v2, jnp.zeros_like(v2)) for h in range(KV_B)], axis=0)
        o = jnp.dot(p_wide, v_heads, preferred_element_type=F32)
        for g in range(G_B):
            o_ref[0, j * w:(j + 1) * w, g * K_W_B:(g + 1) * K_W_B] = (
                o[g * w:(g + 1) * w].astype(o_ref.dtype))


def _attn_b_prompt(q, kb, vb, sink_rows, batch, seq):
    tq = TQ_B
    r = tq // WINDOW
    q3 = q.reshape(batch, seq, Q_W_B)
    k3 = kb.reshape(batch, seq, K_W_B)
    v3 = vb.reshape(batch, seq, V_W_B)
    cur = lambda b, i: (b, i, 0)
    prev = lambda b, i: (b, jnp.maximum(i * r - 1, 0), 0)
    return pl.pallas_call(
        functools.partial(_attn_b_body, tq=tq),
        grid=(batch, seq // tq),
        in_specs=[
            pl.BlockSpec((1, tq, Q_W_B), cur),
            pl.BlockSpec((1, tq, K_W_B), cur),
            pl.BlockSpec((1, WINDOW, K_W_B), prev),
            pl.BlockSpec((1, tq, V_W_B), cur),
            pl.BlockSpec((1, WINDOW, V_W_B), prev),
            pl.BlockSpec(sink_rows.shape, lambda b, i: (0, 0)),
        ],
        out_specs=pl.BlockSpec((1, tq, Q_W_B), cur),
        out_shape=jax.ShapeDtypeStruct((batch, seq, Q_W_B), BF16),
        compiler_params=_cparams(("parallel", "arbitrary")),
        name="attn_b_prompt",
    )(q3, k3, k3, v3, v3, sink_rows)


def _attn_b_sample_body(qbd_ref, kt_ref, vt_ref, knew_ref, vnew_ref, sink_ref,
                        o_ref, kto_ref, vto_ref, *, t_new):
    w = WINDOW
    n_row = qbd_ref.shape[1]
    lane = lax.broadcasted_iota(jnp.int32, (K_W_B, w), 1)
    t = lax.broadcasted_iota(jnp.int32, (n_row, 2 * w), 0) & (t_new - 1)
    col = lax.broadcasted_iota(jnp.int32, (n_row, 2 * w), 1)
    valid = ((col > t) & (col < w)) | ((col >= 2 * w - t_new) & (col <= t + 2 * w - t_new))
    sink_rows = sink_ref[...]

    def new_tile(new_rows):
        pad = jnp.zeros((w - new_rows.shape[0], new_rows.shape[1]), F32)
        return jnp.concatenate([pad, new_rows], axis=0).T

    def shifted(buf_t, new_t):
        return jnp.where(lane < w - t_new, pltpu.roll(buf_t, w - t_new, axis=1), new_t)

    for b in range(qbd_ref.shape[0]):
        kt, vt = kt_ref[b], vt_ref[b]
        knew_t, vnew_t = new_tile(knew_ref[b]), new_tile(vnew_ref[b])
        kto_ref[b] = shifted(kt, knew_t)
        vto_ref[b] = shifted(vt, vnew_t)
        k2t = jnp.concatenate([kt, knew_t], axis=1).astype(BF16)
        v2 = jnp.concatenate([vt, vnew_t], axis=1).T.astype(BF16)
        s = jnp.dot(qbd_ref[b], k2t, preferred_element_type=F32)
        p = _sink_softmax(jnp.where(valid, s, -jnp.inf), sink_rows)
        pv = jnp.dot(p.astype(BF16), v2, preferred_element_type=F32)
        o_ref[b] = _unstack_heads(pv, 1, n_row // KV_B)[0]


def _attn_b_sample(qbd, kbuf_t, vbuf_t, knew, vnew, sink_rows, t_new):
    n_seq, n_row, _ = qbd.shape
    nb = math.gcd(n_seq, SEQS_PER_STEP_B)
    per_seq = lambda s: (s, 0, 0)
    buf_spec = pl.BlockSpec((nb, K_W_B, WINDOW), per_seq)
    buf_shape = jax.ShapeDtypeStruct((n_seq, K_W_B, WINDOW), F32)
    return pl.pallas_call(
        functools.partial(_attn_b_sample_body, t_new=t_new),
        grid=(n_seq // nb,),
        in_specs=[
            pl.BlockSpec((nb, n_row, K_W_B), per_seq),
            buf_spec,
            buf_spec,
            pl.BlockSpec((nb,) + knew.shape[1:], per_seq),
            pl.BlockSpec((nb,) + vnew.shape[1:], per_seq),
            pl.BlockSpec(sink_rows.shape, lambda s: (0, 0)),
        ],
        out_specs=[pl.BlockSpec((nb, n_row // KV_B, K_W_B), per_seq), buf_spec, buf_spec],
        out_shape=[jax.ShapeDtypeStruct((n_seq, n_row // KV_B, K_W_B), F32), buf_shape, buf_shape],
        compiler_params=_cparams(("parallel",)),
        name="attn_b_sample",
    )(qbd, kbuf_t, vbuf_t, knew, vnew, sink_rows)


def _out_b_body(o_ref, gate_ref, x_ref, w_ref, nf_ref, y_ref):
    g = gate_ref[...].astype(F32)
    u = (o_ref[...].astype(F32) * (g * jax.nn.sigmoid(g))).astype(BF16)
    y = x_ref[...] + jnp.dot(u, w_ref[...], preferred_element_type=F32)
    ms = jnp.mean(y * y, axis=-1, keepdims=True)
    y_ref[...] = (y * lax.rsqrt(ms + EPS)) * nf_ref[...]


def _out_b(o, gate, x, w_bf, norm_f):
    n = x.shape[0]
    tm = min(TM_OUT, n)
    row = lambda i: (i, 0)
    fixed = lambda i: (0, 0)
    return pl.pallas_call(
        _out_b_body,
        grid=(n // tm,),
        in_specs=[
            pl.BlockSpec((tm, D_MODEL), row),
            pl.BlockSpec((tm, D_MODEL), row),
            pl.BlockSpec((tm, D_MODEL), row),
            pl.BlockSpec((D_MODEL, D_MODEL), fixed),
            pl.BlockSpec((1, D_MODEL), fixed),
        ],
        out_specs=pl.BlockSpec((tm, D_MODEL), row),
        out_shape=jax.ShapeDtypeStruct((n, D_MODEL), F32),
        compiler_params=_cparams(("parallel",)),
        name="out_b",
    )(o, gate, x, w_bf, norm_f.reshape(1, -1))


def _pad_rows(x, rows):
    return jnp.pad(x, ((0, 0), (0, rows - x.shape[1]), (0, 0)))


def kernel(x_prompt, x_sample, cache_a_k, cache_a_v, page_table, state_b_k, state_b_v,
           norm_a, w_in_a, lambda_q1, lambda_k1, lambda_q2, lambda_k2, subln_a, w_out_a,
           norm_b, w_in_b, b_in_b, sinks_b, w_out_b, norm_f):
    batch, seq, _ = x_prompt.shape
    n_seq, t_new, _ = x_sample.shape
    past = page_table.shape[1] * PAGE_SIZE
    xp = x_prompt.reshape(batch * seq, D_MODEL)
    xs = x_sample.reshape(n_seq * t_new, D_MODEL)

    cos_p, sin_p = _rope_tables(jnp.arange(seq))
    cos_s, sin_s = _rope_tables(past + jnp.arange(t_new))
    cos_s = jnp.tile(cos_s, (n_seq, 1))
    sin_s = jnp.tile(sin_s, (n_seq, 1))
    lams = [v.reshape(1, HD_A) for v in (lambda_q1, lambda_k1, lambda_q2, lambda_k2)]

    wa = w_in_a.astype(BF16)
    ba = jnp.zeros((w_in_a.shape[1],), F32)
    woa = w_out_a.astype(BF16)
    dims_a = dict(q_w=Q_W_A, k_w=K_W_A, v_w=V_W_A)
    q_p, kt_p, vh_p, kb_p, vb_p, gate_p = _proj(
        xp, norm_a, wa, ba, cos_p, sin_p, seq, k_layout="transposed", v_layout="head_rows",
        vb_layout="rows", **dims_a)
    q_s, k_s, vh_s, _, _, gate_s = _proj(
        xs, norm_a, wa, ba, cos_s, sin_s, n_seq * t_new, k_layout="rows", v_layout="head_rows",
        vb_layout="rows", **dims_a)

    qr = q_s.reshape(n_seq, t_new, KV_A, G_A, 2, HD_A).transpose(0, 2, 4, 3, 1, 5)
    qr = qr.reshape(n_seq, 2 * KV_A * G_A * t_new, HD_A)
    blk = jnp.repeat(jnp.arange(2 * KV_A), G_A * t_new)
    onehot = (blk[:, None] == jnp.arange(2 * KV_A)[None, :]).astype(BF16)
    qbd_a = (qr[:, :, None, :] * onehot[None, :, :, None]).reshape(n_seq, -1, K_W_A)
    knew_a = _pad_rows(k_s.reshape(n_seq, t_new, K_W_A), 8)
    vnew_a = _pad_rows(vh_s.reshape(n_seq, t_new, V_W_A), 8)
    o_p, o_s = _attn_a(q_p, kb_p, vb_p, batch, seq,
                       page_table, qbd_a, cache_a_k, cache_a_v, knew_a, vnew_a, lams)
    xp1 = _out_a(o_p.reshape(batch * seq, D_MODEL), gate_p, xp, subln_a, woa)
    o_s = o_s.reshape(n_seq, KV_A, G_A, t_new, VD_A).transpose(0, 3, 1, 2, 4)
    xs1 = _out_a(o_s.reshape(n_seq * t_new, D_MODEL), gate_s, xs, subln_a, woa)

    perm = (jnp.arange(G_B)[:, None, None] * HD_B
            + jnp.arange(KV_B)[None, :, None] * (G_B * HD_B)
            + jnp.arange(HD_B)[None, None, :]).reshape(-1)
    gate0 = Q_W_B + K_W_B + V_W_B
    cols = jnp.concatenate([perm, jnp.arange(Q_W_B, gate0), gate0 + perm])
    wb = w_in_b[:, cols].astype(BF16)
    bb = b_in_b[cols]
    wob = w_out_b[perm, :].astype(BF16)
    dims_b = dict(q_w=Q_W_B, k_w=K_W_B, v_w=V_W_B)
    qb_p, kt_bp, vt_bp, kbb_p, vbb_p, gb_p = _proj(
        xp1, norm_b, wb, bb, cos_p, sin_p, seq, k_layout="transposed", v_layout="transposed",
        vb_layout="rows", **dims_b)
    qb_s, kf_s, vf_s, _, _, gb_s = _proj(
        xs1, norm_b, wb, bb, cos_s, sin_s, n_seq * t_new, k_layout="rows", v_layout="rows",
        vb_layout="rows", **dims_b)

    sink_p = jnp.broadcast_to(jnp.repeat(sinks_b, WINDOW)[:, None], (H_B * WINDOW, LANES))
    ob_p = _attn_b_prompt(qb_p, kbb_p, vbb_p, sink_p, batch, seq)
    y_p = _out_b(ob_p.reshape(batch * seq, D_MODEL), gb_p, xp1, wob, norm_f)

    qr = qb_s.reshape(n_seq, t_new, G_B, KV_B, HD_B).transpose(0, 3, 2, 1, 4)
    qr = qr.reshape(n_seq, KV_B * G_B * t_new, HD_B)
    blk = jnp.repeat(jnp.arange(KV_B), G_B * t_new)
    onehot = (blk[:, None] == jnp.arange(KV_B)[None, :]).astype(BF16)
    qbd_b = (qr[:, :, None, :] * onehot[None, :, :, None]).reshape(n_seq, -1, K_W_B)
    sink_s = jnp.broadcast_to(jnp.repeat(sinks_b, t_new)[:, None], (H_B * t_new, LANES))
    front = ((0, 0), (8 - t_new, 0), (0, 0))
    knew_b = jnp.pad(kf_s.reshape(n_seq, t_new, K_W_B), front)
    vnew_b = jnp.pad(vf_s.reshape(n_seq, t_new, V_W_B), front)
    to_minor = lambda buf: buf.transpose(0, 2, 3, 1).reshape(buf.shape[0], -1, WINDOW)
    from_minor = lambda t: t.reshape(t.shape[0], KV_B, HD_B, WINDOW).transpose(0, 3, 1, 2)
    ob_s, kt_bs, vt_bs = _attn_b_sample(qbd_b, to_minor(state_b_k), to_minor(state_b_v),
                                        knew_b, vnew_b, sink_s, t_new)
    ob_s = ob_s.reshape(n_seq, G_B, t_new, K_W_B).transpose(0, 2, 1, 3)
    y_s = _out_b(ob_s.reshape(n_seq * t_new, D_MODEL), gb_s, xs1, wob, norm_f)

    return (
        y_p.reshape(batch, seq, D_MODEL),
        y_s.reshape(n_seq, t_new, D_MODEL),
        kt_p.reshape(batch, KV_A, 2, HD_A, seq).transpose(0, 4, 1, 2, 3),
        vh_p.reshape(batch, seq, KV_A, VD_A),
        k_s.reshape(n_seq, t_new, KV_A, 2, HD_A),
        vh_s.reshape(n_seq, t_new, KV_A, VD_A),
        from_minor(kt_bp[:, :, seq - WINDOW:]),
        from_minor(vt_bp[:, :, seq - WINDOW:]),
        from_minor(kt_bs),
        from_minor(vt_bs),
    )
```

```python
import functools
import math

import jax
import jax.numpy as jnp
from jax import lax
from jax.experimental import pallas as pl
from jax.experimental.pallas import tpu as pltpu

F32 = jnp.float32
BF16 = jnp.bfloat16

D_MODEL = 1024
PAGE_SIZE = 128
EPS = 1e-5
ROPE_THETA = 500000.0
H_A, HD_A, KV_A = 8, 64, 2
VD_A = 2 * HD_A
G_A = H_A // KV_A
Q_W_A = H_A * 2 * HD_A
K_W_A = KV_A * 2 * HD_A
V_W_A = KV_A * VD_A
LAMBDA_INIT_A = 0.8 - 0.6 * math.exp(-0.3 * 0)
H_B, HD_B, KV_B = 16, 64, 4
G_B = H_B // KV_B
WINDOW = 128
Q_W_B = H_B * HD_B
K_W_B = KV_B * HD_B
V_W_B = KV_B * HD_B
ROT = HD_A // 4
HALF = ROT // 2

LOG2E = math.log2(math.e)
Q_SCALE = (HD_A ** -0.5) * LOG2E

LANES = 128
VMEM_LIMIT = 48 * 1024 * 1024

TM = 512
TM_OUT = 1024
TQ_A = 512
TQ_B = 512
PAGES_PER_STEP = 32
SEQS_PER_STEP_B = 8

NT_DIMS = (((1,), (1,)), ((), ()))


def _lane_tile(x, n):
    return x if n == 1 else jnp.concatenate([x] * n, axis=1)


def _cparams(sem):
    return pltpu.CompilerParams(dimension_semantics=sem, vmem_limit_bytes=VMEM_LIMIT)


def _rope_tables(pos):
    inv = jnp.power(ROPE_THETA, -jnp.arange(HALF, dtype=F32) * 2.0 / ROT)
    ang = pos.astype(F32)[:, None] * inv[None, :]
    cos, sin = jnp.cos(ang), jnp.sin(ang)
    ones = jnp.ones((pos.shape[0], HD_A - ROT), F32)
    cos_h = jnp.concatenate([cos, cos, ones], axis=1)
    sin_h = jnp.concatenate([-sin, sin, 0.0 * ones], axis=1)
    return jnp.tile(cos_h, (1, LANES // HD_A)), jnp.tile(sin_h, (1, LANES // HD_A))


def _proj_body(x_ref, g_ref, w_ref, b_ref, cos_ref, sin_ref,
               q_ref, k_ref, v_ref, kb_ref, vb_ref, gate_ref, *, q_w, k_w, v_w,
               k_layout, v_layout, vb_layout):
    x = x_ref[...]
    ms = jnp.mean(x * x, axis=-1, keepdims=True)
    xn = (x * lax.rsqrt(ms + EPS)) * g_ref[...]
    z = jnp.dot(xn.astype(BF16), w_ref[...], preferred_element_type=F32) + b_ref[...]
    cos = cos_ref[...]
    sin = sin_ref[...]
    lane = lax.broadcasted_iota(jnp.int32, cos.shape, 1)
    first_half = (lane & (HD_A - 1)) < HALF

    def rope(blk):
        partner = jnp.where(first_half, pltpu.roll(blk, LANES - HALF, axis=1),
                            pltpu.roll(blk, HALF, axis=1))
        return blk * cos + partner * sin

    for j in range(q_w // LANES):
        sl = slice(j * LANES, (j + 1) * LANES)
        q_ref[:, sl] = (rope(z[:, sl]) * Q_SCALE).astype(BF16)
    kr = jnp.concatenate(
        [rope(z[:, q_w + j * LANES:q_w + (j + 1) * LANES]) for j in range(k_w // LANES)], axis=1)
    v = z[:, q_w + k_w:q_w + k_w + v_w]
    kb_ref[...] = kr.astype(BF16)
    gate_ref[...] = z[:, q_w + k_w + v_w:].astype(gate_ref.dtype)
    tm = kr.shape[0]
    if k_layout == "transposed":
        k_ref[0] = kr.T
    else:
        k_ref[...] = kr
    if "transposed" in (v_layout, vb_layout):
        vt = v.T
    if vb_layout == "transposed":
        vb_ref[0] = vt.astype(BF16)
    else:
        vb_ref[...] = v.astype(BF16)
    if v_layout == "transposed":
        v_ref[0] = vt
    elif v_layout == "head_rows":
        n_h = v_w // LANES
        for h in range(n_h):
            v_ref[pl.ds(h, tm, stride=n_h), :] = v[:, h * LANES:(h + 1) * LANES]
    else:
        v_ref[...] = v


def _proj(x, norm_g, w_bf, bias, cos_t, sin_t, seq, *, q_w, k_w, v_w,
          k_layout, v_layout, vb_layout):
    n = x.shape[0]
    tm = min(TM, n, seq)
    width = w_bf.shape[1]
    gate_w = width - q_w - k_w - v_w
    n_tab = cos_t.shape[0] // tm
    per_seq = seq // tm
    row = lambda i: (i, 0)
    fixed = lambda i: (0, 0)
    tab = lambda i: (i % n_tab, 0)

    def out(w, layout, dtype):
        if layout == "transposed":
            return (pl.BlockSpec((1, w, tm), lambda i: (i // per_seq, 0, i % per_seq)),
                    jax.ShapeDtypeStruct((n // seq, w, seq), dtype))
        if layout == "head_rows":
            n_h = w // LANES
            return (pl.BlockSpec((tm * n_h, LANES), row),
                    jax.ShapeDtypeStruct((n * n_h, LANES), dtype))
        return pl.BlockSpec((tm, w), row), jax.ShapeDtypeStruct((n, w), dtype)

    k_spec, k_shape = out(k_w, k_layout, F32)
    v_spec, v_shape = out(v_w, v_layout, F32)
    vb_spec, vb_shape = out(v_w, vb_layout, BF16)
    return pl.pallas_call(
        functools.partial(_proj_body, q_w=q_w, k_w=k_w, v_w=v_w,
                          k_layout=k_layout, v_layout=v_layout, vb_layout=vb_layout),
        grid=(n // tm,),
        in_specs=[
            pl.BlockSpec((tm, D_MODEL), row),
            pl.BlockSpec((1, D_MODEL), fixed),
            pl.BlockSpec((D_MODEL, width), fixed),
            pl.BlockSpec((1, width), fixed),
            pl.BlockSpec((tm, LANES), tab),
            pl.BlockSpec((tm, LANES), tab),
        ],
        out_specs=[
            pl.BlockSpec((tm, q_w), row),
            k_spec,
            v_spec,
            pl.BlockSpec((tm, k_w), row),
            vb_spec,
            pl.BlockSpec((tm, gate_w), row),
        ],
        out_shape=[
            jax.ShapeDtypeStruct((n, q_w), BF16),
            k_shape,
            v_shape,
            jax.ShapeDtypeStruct((n, k_w), BF16),
            vb_shape,
            jax.ShapeDtypeStruct((n, gate_w), BF16),
        ],
        compiler_params=_cparams(("parallel",)),
        name="proj",
    )(x, norm_g.reshape(1, -1), w_bf, bias.reshape(1, -1), cos_t, sin_t)


def _lambda_full(lq1_ref, lk1_ref, lq2_ref, lk2_ref):
    s1 = jnp.sum(lq1_ref[...] * lk1_ref[...], axis=1, keepdims=True)
    s2 = jnp.sum(lq2_ref[...] * lk2_ref[...], axis=1, keepdims=True)
    return jnp.exp(s1) - jnp.exp(s2) + LAMBDA_INIT_A


def _attn_a_body(b_ref, qi_ref, kj_ref, pt_ref,
                 q_ref, k_ref, v_ref, qbd_ref, k_hbm, v_hbm, knew_ref, vnew_ref,
                 lq1_ref, lk1_ref, lq2_ref, lk2_ref,
                 o_ref, os_ref,
                 qs_ref, m_ref, acc_ref, kbuf, vbuf, sem, ms_ref, ls_ref, accs_ref,
                 *, tq, n_flash, n_paged, n_pg, n_c, t_new):
    step = pl.program_id(0)
    qi = qi_ref[step]
    kj = kj_ref[step]
    flash_on = step < n_flash
    paged_on = step < n_paged
    c = lax.rem(step, n_c)
    slot = step & 1
    per_head = 2 * G_A
    qbd = qbd_ref[0]
    n_row = qbd.shape[0]
    h_rows = n_row // KV_A

    def page_copies(page_of, sl):
        cps = []
        for i in range(n_pg):
            page = page_of(i)
            cps.append(pltpu.make_async_copy(k_hbm.at[page], kbuf.at[sl, i], sem.at[sl]))
            cps.append(pltpu.make_async_copy(v_hbm.at[page], vbuf.at[sl, i], sem.at[sl]))
        return cps

    @pl.when(step == 0)
    def _first_fetch():
        for cp in page_copies(lambda i: pt_ref[i], 0):
            cp.start()

    @pl.when(step + 1 < n_paged)
    def _prefetch_next():
        for cp in page_copies(lambda i: pt_ref[(step + 1) * n_pg + i], 1 - slot):
            cp.start()

    @pl.when(paged_on)
    def _wait_pages():
        for cp in page_copies(lambda i: 0, slot):
            cp.wait()

    @pl.when(flash_on & (kj == 0))
    def _init_flash():
        lane = lax.broadcasted_iota(jnp.int32, (tq, LANES), 1)
        for h in range(KV_A):
            for g in range(G_A):
                slab = q_ref[0, :, (h * G_A + g) * LANES:(h * G_A + g + 1) * LANES]
                zero = jnp.zeros_like(slab)
                qs_ref[h * per_head + g] = jnp.where(lane < HD_A, slab, zero)
                qs_ref[h * per_head + G_A + g] = jnp.where(lane >= HD_A, slab, zero)
        m_ref[...] = jnp.full(m_ref.shape, -jnp.inf, F32)
        acc_ref[...] = jnp.zeros(acc_ref.shape, F32)

    @pl.when(paged_on & (c == 0))
    def _init_paged():
        ms_ref[...] = jnp.full(ms_ref.shape, -jnp.inf, F32)
        ls_ref[...] = jnp.zeros(ls_ref.shape, F32)
        accs_ref[...] = jnp.zeros(accs_ref.shape, F32)

    def flash_update(masked):
        ones = jnp.ones((tq, LANES), BF16)
        for h in range(KV_A):
            k = k_ref[0, :, h * LANES:(h + 1) * LANES]
            v_ones = jnp.concatenate([v_ref[0, :, h * VD_A:(h + 1) * VD_A], ones], axis=1)
            for i in range(h * per_head, (h + 1) * per_head):
                s = lax.dot_general(qs_ref[i], k, NT_DIMS, preferred_element_type=F32)
                if masked:
                    row = lax.broadcasted_iota(jnp.int32, s.shape, 0)
                    col = lax.broadcasted_iota(jnp.int32, s.shape, 1)
                    s = jnp.where(col <= row, s, -jnp.inf)
                m_prev = m_ref[i]
                m_new = jnp.maximum(m_prev, jnp.max(s, axis=1, keepdims=True))
                alpha = jnp.exp2(m_prev - m_new)
                p = jnp.exp2(s - _lane_tile(m_new, tq // LANES)).astype(BF16)
                acc_ref[i] = (_lane_tile(alpha, 2) * acc_ref[i]
                              + jnp.dot(p, v_ones, preferred_element_type=F32))
                m_ref[i] = m_new

    def flash_finish():
        lam = _lambda_full(lq1_ref, lk1_ref, lq2_ref, lk2_ref)
        for h in range(KV_A):
            for g in range(G_A):
                a1 = acc_ref[h * per_head + g]
                a2 = acc_ref[h * per_head + G_A + g]
                o1 = a1[:, :VD_A] / a1[:, VD_A:]
                o2 = a2[:, :VD_A] / a2[:, VD_A:]
                o_ref[0, :, (h * G_A + g) * LANES:(h * G_A + g + 1) * LANES] = (
                    o1 - lam * o2).astype(o_ref.dtype)

    def online_update(s, v_of, width):
        m_prev = ms_ref[...]
        m_new = jnp.maximum(m_prev, jnp.max(s, axis=1, keepdims=True))
        alpha = jnp.exp2(m_prev - m_new)
        p = jnp.exp2(s - _lane_tile(m_new, s.shape[1] // LANES))
        ls_ref[...] = alpha * ls_ref[...] + jnp.sum(p, axis=1, keepdims=True)
        p = p.astype(BF16)
        pvs = []
        for h in range(KV_A):
            pv = jnp.zeros((h_rows, VD_A), F32)
            for i in range(s.shape[1] // width):
                pv = pv + jnp.dot(p[h * h_rows:(h + 1) * h_rows, i * width:(i + 1) * width],
                                  v_of(i, h), preferred_element_type=F32)
            pvs.append(pv)
        accs_ref[...] = alpha * accs_ref[...] + jnp.concatenate(pvs, axis=0)
        ms_ref[...] = m_new

    def paged_update():
        def k_pair(j):
            return jnp.concatenate([kbuf[slot, 2 * j], kbuf[slot, 2 * j + 1]], axis=1).astype(BF16)

        def v_pair(j, h):
            return jnp.concatenate(
                [vbuf[slot, 2 * j + r, pl.ds(h, PAGE_SIZE, stride=KV_A), :] for r in range(2)],
                axis=0).astype(BF16)

        s = jnp.concatenate(
            [jnp.dot(qbd, k_pair(j), preferred_element_type=F32) for j in range(n_pg // 2)], axis=1)
        online_update(jnp.where(paged_on, s, -jnp.inf), v_pair, 2 * PAGE_SIZE)

    @pl.when(flash_on & (kj < qi))
    def _off_diagonal():
        flash_update(False)
        paged_update()

    @pl.when(flash_on & (kj == qi))
    def _diagonal():
        flash_update(True)
        flash_finish()
        paged_update()

    @pl.when(jnp.logical_not(flash_on))
    def _sample_only():
        paged_update()

    @pl.when(paged_on & (c == n_c - 1))
    def _paged_finish():
        pad = jnp.zeros((PAGE_SIZE - knew_ref.shape[1], knew_ref.shape[2]), F32)
        kn = jnp.concatenate([knew_ref[0], pad], axis=0).astype(BF16)
        vn = jnp.concatenate([vnew_ref[0], pad], axis=0).astype(BF16)
        sn = lax.dot_general(qbd, kn, NT_DIMS, preferred_element_type=F32)
        t = lax.broadcasted_iota(jnp.int32, sn.shape, 0) & (t_new - 1)
        col = lax.broadcasted_iota(jnp.int32, sn.shape, 1)
        online_update(jnp.where(col <= t, sn, -jnp.inf),
                      lambda i, h: vn[:, h * VD_A:(h + 1) * VD_A], PAGE_SIZE)
        lam = _lambda_full(lq1_ref, lk1_ref, lq2_ref, lk2_ref)
        o = accs_ref[...] / ls_ref[...]
        half = h_rows // 2
        os_ref[0] = jnp.concatenate(
            [o[h * h_rows:h * h_rows + half] - lam * o[h * h_rows + half:(h + 1) * h_rows]
             for h in range(KV_A)], axis=0)


def _attn_a(q, kb, vb, batch, seq, page_table, qbd, cache_k, cache_v, knew, vnew, lams):
    tq = TQ_A
    nq = seq // tq
    n_seq, n_pages = page_table.shape
    n_pool = cache_k.shape[0]
    n_pg = PAGES_PER_STEP
    n_c = n_pages // n_pg
    n_row = qbd.shape[1]
    t_new = n_row // (2 * KV_A * G_A)
    pairs = [(b, i, j) for b in range(batch) for i in range(nq) for j in range(i + 1)]
    n_flash, n_paged = len(pairs), n_seq * n_c
    n_steps = max(n_flash, n_paged)
    pairs = pairs + [pairs[-1]] * (n_steps - n_flash)
    b_tab, qi_tab, kj_tab = (jnp.asarray([p[a] for p in pairs], jnp.int32) for a in range(3))
    q3 = q.reshape(batch, seq, Q_W_A)
    k3 = kb.reshape(batch, seq, K_W_A)
    v3 = vb.reshape(batch, seq, V_W_A)
    ck = jnp.transpose(cache_k, (0, 2, 3, 4, 1)).reshape(n_pool, K_W_A, PAGE_SIZE)
    cv = cache_v.reshape(n_pool, PAGE_SIZE * KV_A, VD_A)
    n_slab = KV_A * 2 * G_A
    q_rows = lambda s, b, qi, kj, pt: (b[s], qi[s], 0)
    k_rows = lambda s, b, qi, kj, pt: (b[s], kj[s], 0)
    per_seq = lambda s, b, qi, kj, pt: (jnp.minimum(s // n_c, n_seq - 1), 0, 0)
    lam_spec = pl.BlockSpec((1, HD_A), lambda s, b, qi, kj, pt: (0, 0))
    hbm = pl.BlockSpec(memory_space=pl.ANY)
    grid_spec = pltpu.PrefetchScalarGridSpec(
        num_scalar_prefetch=4,
        grid=(n_steps,),
        in_specs=[
            pl.BlockSpec((1, tq, Q_W_A), q_rows),
            pl.BlockSpec((1, tq, K_W_A), k_rows),
            pl.BlockSpec((1, tq, V_W_A), k_rows),
            pl.BlockSpec((1, n_row, K_W_A), per_seq), hbm, hbm,
            pl.BlockSpec((1,) + knew.shape[1:], per_seq),
            pl.BlockSpec((1,) + vnew.shape[1:], per_seq),
            lam_spec, lam_spec, lam_spec, lam_spec,
        ],
        out_specs=[pl.BlockSpec((1, tq, H_A * VD_A), q_rows),
                   pl.BlockSpec((1, n_row // 2, VD_A), per_seq)],
        scratch_shapes=[
            pltpu.VMEM((n_slab, tq, LANES), BF16),
            pltpu.VMEM((n_slab, tq, LANES), F32),
            pltpu.VMEM((n_slab, tq, 2 * VD_A), F32),
            pltpu.VMEM((2, n_pg, K_W_A, PAGE_SIZE), F32),
            pltpu.VMEM((2, n_pg, PAGE_SIZE * KV_A, VD_A), F32),
            pltpu.SemaphoreType.DMA((2,)),
            pltpu.VMEM((n_row, LANES), F32),
            pltpu.VMEM((n_row, LANES), F32),
            pltpu.VMEM((n_row, VD_A), F32),
        ],
    )
    return pl.pallas_call(
        functools.partial(_attn_a_body, tq=tq, n_flash=n_flash, n_paged=n_paged,
                          n_pg=n_pg, n_c=n_c, t_new=t_new),
        grid_spec=grid_spec,
        out_shape=[jax.ShapeDtypeStruct((batch, seq, H_A * VD_A), BF16),
                   jax.ShapeDtypeStruct((n_seq, n_row // 2, VD_A), F32)],
        compiler_params=_cparams(("arbitrary",)),
        name="attn_a",
    )(b_tab, qi_tab, kj_tab, page_table.reshape(-1), q3, k3, v3, qbd, ck, cv, knew, vnew, *lams)


def _out_a_body(o_ref, gate_ref, x_ref, sg_ref, w_ref, y_ref):
    sg = sg_ref[...]
    us = []
    for j in range(H_A):
        sl = slice(j * VD_A, (j + 1) * VD_A)
        oj = o_ref[:, sl].astype(F32)
        ms = jnp.mean(oj * oj, axis=-1, keepdims=True)
        yj = ((oj * lax.rsqrt(ms + EPS)) * sg) * (1.0 - LAMBDA_INIT_A)
        gj = gate_ref[:, sl].astype(F32)
        us.append((yj * (gj * jax.nn.sigmoid(gj))).astype(BF16))
    u = jnp.concatenate(us, axis=1)
    y_ref[...] = x_ref[...] + jnp.dot(u, w_ref[...], preferred_element_type=F32)


def _out_a(o, gate, x, subln_g, w_bf):
    n = x.shape[0]
    tm = min(TM_OUT, n)
    row = lambda i: (i, 0)
    fixed = lambda i: (0, 0)
    return pl.pallas_call(
        _out_a_body,
        grid=(n // tm,),
        in_specs=[
            pl.BlockSpec((tm, D_MODEL), row),
            pl.BlockSpec((tm, D_MODEL), row),
            pl.BlockSpec((tm, D_MODEL), row),
            pl.BlockSpec((1, VD_A), fixed),
            pl.BlockSpec((D_MODEL, D_MODEL), fixed),
        ],
        out_specs=pl.BlockSpec((tm, D_MODEL), row),
        out_shape=jax.ShapeDtypeStruct((n, D_MODEL), F32),
        compiler_params=_cparams(("parallel",)),
        name="out_a",
    )(o, gate, x, subln_g.reshape(1, -1), w_bf)


def _sink_softmax(s, sink_rows):
    sk = sink_rows * LOG2E
    m = jnp.maximum(jnp.max(s, axis=1, keepdims=True), sk)
    e = jnp.exp2(s - _lane_tile(m, s.shape[1] // LANES))
    den = jnp.sum(e, axis=1, keepdims=True) + jnp.exp2(sk - m)
    return e / _lane_tile(den, s.shape[1] // LANES)


def _stack_heads(slabs, n_t):
    lane_head = lax.broadcasted_iota(jnp.int32, (n_t, K_W_B), 1) >> (HD_B.bit_length() - 1)
    return jnp.concatenate(
        [jnp.where(lane_head == h, slabs[g], jnp.zeros_like(slabs[g]))
         for h in range(KV_B) for g in range(len(slabs))], axis=0)


def _unstack_heads(pv, n_g, n_t):
    lane_head = lax.broadcasted_iota(jnp.int32, (n_t, K_W_B), 1) >> (HD_B.bit_length() - 1)
    outs = []
    for g in range(n_g):
        acc = jnp.zeros((n_t, K_W_B), F32)
        for h in range(KV_B):
            r0 = (h * n_g + g) * n_t
            acc = jnp.where(lane_head == h, pv[r0:r0 + n_t], acc)
        outs.append(acc)
    return outs


def _attn_b_body(q_ref, kc_ref, kp_ref, vc_ref, vp_ref, sink_ref, gate_ref, x_ref, w_ref, nf_ref,
                 y_ref, o_ref, *, tq):
    first_tile = pl.program_id(1) == 0
    w = WINDOW
    head_rows = G_B * w
    sink_rows = sink_ref[...]
    t = lax.broadcasted_iota(jnp.int32, (w, 2 * w), 0)
    col = lax.broadcasted_iota(jnp.int32, (w, 2 * w), 1)
    in_window = (col > t) & (col <= t + w)
    bias = jnp.where(in_window, 0.0, -jnp.inf)
    bias_no_prev = jnp.where(in_window & (col >= w), 0.0, -jnp.inf)
    lane_head = lax.broadcasted_iota(jnp.int32, (2 * w, V_W_B), 1) >> (HD_B.bit_length() - 1)
    for j in range(tq // w):
        if j == 0:
            k2 = jnp.concatenate([kp_ref[0], kc_ref[0, :w]], axis=0)
            v2 = jnp.concatenate([vp_ref[0], vc_ref[0, :w]], axis=0)
            b = jnp.where(first_tile, bias_no_prev, bias)
        else:
            k2 = kc_ref[0, (j - 1) * w:(j + 1) * w]
            v2 = vc_ref[0, (j - 1) * w:(j + 1) * w]
            b = bias
        qsub = q_ref[0, j * w:(j + 1) * w, :]
        qst = _stack_heads([qsub[:, g * K_W_B:(g + 1) * K_W_B] for g in range(G_B)], w)
        s = lax.dot_general(qst, k2, NT_DIMS, preferred_element_type=F32)
        p = _sink_softmax(s + jnp.tile(b, (H_B, 1)), sink_rows).astype(BF16)
        p_wide = jnp.concatenate(
            [p[h * head_rows:(h + 1) * head_rows] for h in range(KV_B)], axis=1)
        v_heads = jnp.concatenate(
            [jnp.where(lane_head == h, v2, jnp.zeros_like(v2)) for h in range(KV_B)], axis=0)
        o = jnp.dot(p_wide, v_heads, preferred_element_type=F32)
        for g in range(G_B):
            o_ref[j * w:(j + 1) * w, g * K_W_B:(g + 1) * K_W_B] = (
                o[g * w:(g + 1) * w].astype(o_ref.dtype))
    _out_b_math(o_ref, gate_ref.at[0], x_ref.at[0], w_ref, nf_ref, y_ref.at[0])


def _attn_b_prompt(q, kb, vb, sink_rows, gate, x, w_bf, norm_f, batch, seq):
    tq = TQ_B
    r = tq // WINDOW
    q3 = q.reshape(batch, seq, Q_W_B)
    k3 = kb.reshape(batch, seq, K_W_B)
    v3 = vb.reshape(batch, seq, V_W_B)
    cur = lambda b, i: (b, i, 0)
    prev = lambda b, i: (b, jnp.maximum(i * r - 1, 0), 0)
    return pl.pallas_call(
        functools.partial(_attn_b_body, tq=tq),
        grid=(batch, seq // tq),
        in_specs=[
            pl.BlockSpec((1, tq, Q_W_B), cur),
            pl.BlockSpec((1, tq, K_W_B), cur),
            pl.BlockSpec((1, WINDOW, K_W_B), prev),
            pl.BlockSpec((1, tq, V_W_B), cur),
            pl.BlockSpec((1, WINDOW, V_W_B), prev),
            pl.BlockSpec(sink_rows.shape, lambda b, i: (0, 0)),
            pl.BlockSpec((1, tq, D_MODEL), cur),
            pl.BlockSpec((1, tq, D_MODEL), cur),
            pl.BlockSpec((D_MODEL, D_MODEL), lambda b, i: (0, 0)),
            pl.BlockSpec((1, D_MODEL), lambda b, i: (0, 0)),
        ],
        out_specs=pl.BlockSpec((1, tq, D_MODEL), cur),
        out_shape=jax.ShapeDtypeStruct((batch, seq, D_MODEL), F32),
        scratch_shapes=[pltpu.VMEM((tq, Q_W_B), BF16)],
        compiler_params=_cparams(("parallel", "arbitrary")),
        name="attn_b_prompt",
    )(q3, k3, k3, v3, v3, sink_rows, gate.reshape(batch, seq, D_MODEL),
      x.reshape(batch, seq, D_MODEL), w_bf, norm_f.reshape(1, -1))


def _attn_b_sample_body(qbd_ref, kt_ref, vt_ref, knew_ref, vnew_ref, sink_ref,
                        o_ref, kto_ref, vto_ref, *, t_new):
    w = WINDOW
    n_row = qbd_ref.shape[1]
    lane = lax.broadcasted_iota(jnp.int32, (K_W_B, w), 1)
    t = lax.broadcasted_iota(jnp.int32, (n_row, 2 * w), 0) & (t_new - 1)
    col = lax.broadcasted_iota(jnp.int32, (n_row, 2 * w), 1)
    valid = ((col > t) & (col < w)) | ((col >= 2 * w - t_new) & (col <= t + 2 * w - t_new))
    sink_rows = sink_ref[...]

    def new_tile(new_rows):
        pad = jnp.zeros((w - new_rows.shape[0], new_rows.shape[1]), F32)
        return jnp.concatenate([pad, new_rows], axis=0).T

    def shifted(buf_t, new_t):
        return jnp.where(lane < w - t_new, pltpu.roll(buf_t, w - t_new, axis=1), new_t)

    for b in range(qbd_ref.shape[0]):
        kt, vt = kt_ref[b], vt_ref[b]
        knew_t, vnew_t = new_tile(knew_ref[b]), new_tile(vnew_ref[b])
        kto_ref[b] = shifted(kt, knew_t)
        vto_ref[b] = shifted(vt, vnew_t)
        k2t = jnp.concatenate([kt, knew_t], axis=1).astype(BF16)
        v2 = jnp.concatenate([vt, vnew_t], axis=1).T.astype(BF16)
        s = jnp.dot(qbd_ref[b], k2t, preferred_element_type=F32)
        p = _sink_softmax(jnp.where(valid, s, -jnp.inf), sink_rows)
        pv = jnp.dot(p.astype(BF16), v2, preferred_element_type=F32)
        o_ref[b] = _unstack_heads(pv, 1, n_row // KV_B)[0]


def _attn_b_sample(qbd, kbuf_t, vbuf_t, knew, vnew, sink_rows, t_new):
    n_seq, n_row, _ = qbd.shape
    nb = math.gcd(n_seq, SEQS_PER_STEP_B)
    per_seq = lambda s: (s, 0, 0)
    buf_spec = pl.BlockSpec((nb, K_W_B, WINDOW), per_seq)
    buf_shape = jax.ShapeDtypeStruct((n_seq, K_W_B, WINDOW), F32)
    return pl.pallas_call(
        functools.partial(_attn_b_sample_body, t_new=t_new),
        grid=(n_seq // nb,),
        in_specs=[
            pl.BlockSpec((nb, n_row, K_W_B), per_seq),
            buf_spec,
            buf_spec,
            pl.BlockSpec((nb,) + knew.shape[1:], per_seq),
            pl.BlockSpec((nb,) + vnew.shape[1:], per_seq),
            pl.BlockSpec(sink_rows.shape, lambda s: (0, 0)),
        ],
        out_specs=[pl.BlockSpec((nb, n_row // KV_B, K_W_B), per_seq), buf_spec, buf_spec],
        out_shape=[jax.ShapeDtypeStruct((n_seq, n_row // KV_B, K_W_B), F32), buf_shape, buf_shape],
        compiler_params=_cparams(("parallel",)),
        name="attn_b_sample",
    )(qbd, kbuf_t, vbuf_t, knew, vnew, sink_rows)


def _out_b_math(o_ref, gate_ref, x_ref, w_ref, nf_ref, y_ref):
    g = gate_ref[...].astype(F32)
    u = (o_ref[...].astype(F32) * (g * jax.nn.sigmoid(g))).astype(BF16)
    y = x_ref[...] + jnp.dot(u, w_ref[...], preferred_element_type=F32)
    ms = jnp.mean(y * y, axis=-1, keepdims=True)
    y_ref[...] = (y * lax.rsqrt(ms + EPS)) * nf_ref[...]


def _out_b(o, gate, x, w_bf, norm_f):
    n = x.shape[0]
    tm = min(TM_OUT, n)
    row = lambda i: (i, 0)
    fixed = lambda i: (0, 0)
    return pl.pallas_call(
        _out_b_math,
        grid=(n // tm,),
        in_specs=[
            pl.BlockSpec((tm, D_MODEL), row),
            pl.BlockSpec((tm, D_MODEL), row),
            pl.BlockSpec((tm, D_MODEL), row),
            pl.BlockSpec((D_MODEL, D_MODEL), fixed),
            pl.BlockSpec((1, D_MODEL), fixed),
        ],
        out_specs=pl.BlockSpec((tm, D_MODEL), row),
        out_shape=jax.ShapeDtypeStruct((n, D_MODEL), F32),
        compiler_params=_cparams(("parallel",)),
        name="out_b",
    )(o, gate, x, w_bf, norm_f.reshape(1, -1))


def _pad_rows(x, rows):
    return jnp.pad(x, ((0, 0), (0, rows - x.shape[1]), (0, 0)))


def kernel(x_prompt, x_sample, cache_a_k, cache_a_v, page_table, state_b_k, state_b_v,
           norm_a, w_in_a, lambda_q1, lambda_k1, lambda_q2, lambda_k2, subln_a, w_out_a,
           norm_b, w_in_b, b_in_b, sinks_b, w_out_b, norm_f):
    batch, seq, _ = x_prompt.shape
    n_seq, t_new, _ = x_sample.shape
    past = page_table.shape[1] * PAGE_SIZE
    xp = x_prompt.reshape(batch * seq, D_MODEL)
    xs = x_sample.reshape(n_seq * t_new, D_MODEL)

    cos_p, sin_p = _rope_tables(jnp.arange(seq))
    cos_s, sin_s = _rope_tables(past + jnp.arange(t_new))
    cos_s = jnp.tile(cos_s, (n_seq, 1))
    sin_s = jnp.tile(sin_s, (n_seq, 1))
    lams = [v.reshape(1, HD_A) for v in (lambda_q1, lambda_k1, lambda_q2, lambda_k2)]

    wa = w_in_a.astype(BF16)
    ba = jnp.zeros((w_in_a.shape[1],), F32)
    woa = w_out_a.astype(BF16)
    dims_a = dict(q_w=Q_W_A, k_w=K_W_A, v_w=V_W_A)
    q_p, kt_p, vh_p, kb_p, vb_p, gate_p = _proj(
        xp, norm_a, wa, ba, cos_p, sin_p, seq, k_layout="transposed", v_layout="head_rows",
        vb_layout="rows", **dims_a)
    q_s, k_s, vh_s, _, _, gate_s = _proj(
        xs, norm_a, wa, ba, cos_s, sin_s, n_seq * t_new, k_layout="rows", v_layout="head_rows",
        vb_layout="rows", **dims_a)

    qr = q_s.reshape(n_seq, t_new, KV_A, G_A, 2, HD_A).transpose(0, 2, 4, 3, 1, 5)
    qr = qr.reshape(n_seq, 2 * KV_A * G_A * t_new, HD_A)
    blk = jnp.repeat(jnp.arange(2 * KV_A), G_A * t_new)
    onehot = (blk[:, None] == jnp.arange(2 * KV_A)[None, :]).astype(BF16)
    qbd_a = (qr[:, :, None, :] * onehot[None, :, :, None]).reshape(n_seq, -1, K_W_A)
    knew_a = _pad_rows(k_s.reshape(n_seq, t_new, K_W_A), 8)
    vnew_a = _pad_rows(vh_s.reshape(n_seq, t_new, V_W_A), 8)
    o_p, o_s = _attn_a(q_p, kb_p, vb_p, batch, seq,
                       page_table, qbd_a, cache_a_k, cache_a_v, knew_a, vnew_a, lams)
    xp1 = _out_a(o_p.reshape(batch * seq, D_MODEL), gate_p, xp, subln_a, woa)
    o_s = o_s.reshape(n_seq, KV_A, G_A, t_new, VD_A).transpose(0, 3, 1, 2, 4)
    xs1 = _out_a(o_s.reshape(n_seq * t_new, D_MODEL), gate_s, xs, subln_a, woa)

    perm = (jnp.arange(G_B)[:, None, None] * HD_B
            + jnp.arange(KV_B)[None, :, None] * (G_B * HD_B)
            + jnp.arange(HD_B)[None, None, :]).reshape(-1)
    gate0 = Q_W_B + K_W_B + V_W_B
    cols = jnp.concatenate([perm, jnp.arange(Q_W_B, gate0), gate0 + perm])
    wb = w_in_b[:, cols].astype(BF16)
    bb = b_in_b[cols]
    wob = w_out_b[perm, :].astype(BF16)
    dims_b = dict(q_w=Q_W_B, k_w=K_W_B, v_w=V_W_B)
    qb_p, kt_bp, vt_bp, kbb_p, vbb_p, gb_p = _proj(
        xp1, norm_b, wb, bb, cos_p, sin_p, seq, k_layout="transposed", v_layout="transposed",
        vb_layout="rows", **dims_b)
    qb_s, kf_s, vf_s, _, _, gb_s = _proj(
        xs1, norm_b, wb, bb, cos_s, sin_s, n_seq * t_new, k_layout="rows", v_layout="rows",
        vb_layout="rows", **dims_b)

    sink_p = jnp.broadcast_to(jnp.repeat(sinks_b, WINDOW)[:, None], (H_B * WINDOW, LANES))
    y_p = _attn_b_prompt(qb_p, kbb_p, vbb_p, sink_p, gb_p, xp1, wob, norm_f, batch, seq)

    qr = qb_s.reshape(n_seq, t_new, G_B, KV_B, HD_B).transpose(0, 3, 2, 1, 4)
    qr = qr.reshape(n_seq, KV_B * G_B * t_new, HD_B)
    blk = jnp.repeat(jnp.arange(KV_B), G_B * t_new)
    onehot = (blk[:, None] == jnp.arange(KV_B)[None, :]).astype(BF16)
    qbd_b = (qr[:, :, None, :] * onehot[None, :, :, None]).reshape(n_seq, -1, K_W_B)
    sink_s = jnp.broadcast_to(jnp.repeat(sinks_b, t_new)[:, None], (H_B * t_new, LANES))
    front = ((0, 0), (8 - t_new, 0), (0, 0))
    knew_b = jnp.pad(kf_s.reshape(n_seq, t_new, K_W_B), front)
    vnew_b = jnp.pad(vf_s.reshape(n_seq, t_new, V_W_B), front)
    to_minor = lambda buf: buf.transpose(0, 2, 3, 1).reshape(buf.shape[0], -1, WINDOW)
    from_minor = lambda t: t.reshape(t.shape[0], KV_B, HD_B, WINDOW).transpose(0, 3, 1, 2)
    ob_s, kt_bs, vt_bs = _attn_b_sample(qbd_b, to_minor(state_b_k), to_minor(state_b_v),
                                        knew_b, vnew_b, sink_s, t_new)
    ob_s = ob_s.reshape(n_seq, G_B, t_new, K_W_B).transpose(0, 2, 1, 3)
    y_s = _out_b(ob_s.reshape(n_seq * t_new, D_MODEL), gb_s, xs1, wob, norm_f)

    return (
        y_p.reshape(batch, seq, D_MODEL),
        y_s.reshape(n_seq, t_new, D_MODEL),
        kt_p.reshape(batch, KV_A, 2, HD_A, seq).transpose(0, 4, 1, 2, 3),
        vh_p.reshape(batch, seq, KV_A, VD_A),
        k_s.reshape(n_seq, t_new, KV_A, 2, HD_A),
        vh_s.reshape(n_seq, t_new, KV_A, VD_A),
        from_minor(kt_bp[:, :, seq - WINDOW:]),
        from_minor(vt_bp[:, :, seq - WINDOW:]),
        from_minor(kt_bs),
        from_minor(vt_bs),
    )
```
